```python
import math
import jax
import jax.numpy as jnp
from jax import lax
import numpy as np

D_MODEL = 2048
BATCH = 1
SEQ = 8192
DEPTH = 2
DEC_BATCH = 32
DEC_SEQ = 1
PAST_LEN = 8192
PAGE_SIZE = 128

HEAD_DIM = 128
NSA_HEADS = 8
NSA_KV = 2
NSA_GROUP = NSA_HEADS // NSA_KV
CMP_STRIDE = 16
CMP_LEN = 2 * CMP_STRIDE
SEL_BLOCK = 64
CMP_PER_SEL = SEL_BLOCK // CMP_STRIDE
N_SEL = 16
WINDOW = 512
DIFF_HEADS = 8
DIFF_QK = HEAD_DIM // 2
FOX_HEADS = 16
FORGET_BIAS_MEAN = 2.0
N_MEM = 256
MEM_HEADS = 4
MEM_WIDTH = MEM_HEADS * HEAD_DIM
N_BUCKETS = 32
MAX_DISTANCE = 128
N_EXPERTS = 64
N_EXPERT_GROUPS = 8
TOPK_GROUPS = 4
TOP_K = 8
D_EXPERT = 512
ROUTE_SCALE = 2.5
Q_BLOCK = 128
N_EVEN = (DEPTH + 1) // 2
N_ODD = DEPTH // 2
DEEPNORM_ALPHA = (2 * DEPTH) ** 0.25
DEEPNORM_BETA = (8 * DEPTH) ** -0.25
LN_EPS = 1e-5
NEG_INF = -1e30
EVEN_WIDTHS = (NSA_HEADS * HEAD_DIM,) + (NSA_KV * HEAD_DIM,) * 6 + (3 * NSA_HEADS,) + (DIFF_HEADS * HEAD_DIM,) * 3
ODD_WIDTHS = (FOX_HEADS * HEAD_DIM,) * 3 + (FOX_HEADS,)

kernel_name = 'hybrid_nsa_diff_fox_moe_step'


def split_cols(p, widths):
    out, off = [], 0
    for w in widths:
        out.append(p[..., off:off + w])
        off += w
    return out


def layer_norm(x, g, b):
    xf = x.astype(jnp.float32)
    mu = jnp.mean(xf, -1, keepdims=True)
    var = jnp.mean(jnp.square(xf - mu), -1, keepdims=True)
    y = (xf - mu) * lax.rsqrt(var + LN_EPS) * g.astype(jnp.float32) + b.astype(jnp.float32)
    return y.astype(x.dtype)


def masked_softmax(logits, mask):
    logits = jnp.where(mask, logits.astype(jnp.float32), NEG_INF)
    m = jnp.max(logits, -1, keepdims=True)
    e = jnp.where(mask, jnp.exp(logits - m), 0.0)
    return e / jnp.maximum(jnp.sum(e, -1, keepdims=True), 1e-30)


def t5_bucket(rel):
    n = jnp.maximum(rel, 0)
    exact = N_BUCKETS // 2
    nf = jnp.maximum(n, 1).astype(jnp.float32)
    large = exact + (jnp.log(nf / exact) / math.log(MAX_DISTANCE / exact) * (N_BUCKETS - exact)).astype(jnp.int32)
    return jnp.where(n < exact, n, jnp.minimum(large, N_BUCKETS - 1))


def sweep_queries(block_fn, n_q, qb):
    out = lax.map(block_fn, jnp.arange(n_q // qb) * qb)
    nb, b, _, c = out.shape
    return jnp.swapaxes(out, 0, 1).reshape(b, nb * qb, c)


def gather_pages(pool, layer_idx, pages):
    return pool[layer_idx, pages].reshape((-1,) + pool.shape[3:])


def compress(k, pos_emb, w):
    b, l = k.shape[:2]
    halves = k.reshape(b, l // CMP_STRIDE, CMP_STRIDE, NSA_KV, HEAD_DIM)
    blocks = jnp.concatenate([halves[:, :-1], halves[:, 1:]], axis=2)
    return jnp.einsum('bclgd,glde->bcge', blocks + pos_emb, w)


def nsa_mixer(q, gates, q_start, kc_raw, vc_raw, ks, vs, kw_all, vw_all, cmp_pos, cmp_w, table):
    b, t = q.shape[:2]
    l = ks.shape[1]
    lp = -(-l // SEL_BLOCK) * SEL_BLOCK
    pad = ((0, 0), (0, lp - l), (0, 0), (0, 0))
    kc = compress(jnp.pad(kc_raw, pad), cmp_pos[0], cmp_w[0])
    vc = compress(jnp.pad(vc_raw, pad), cmp_pos[1], cmp_w[1])
    n_cmp = kc.shape[1]
    c_end = jnp.arange(n_cmp) * CMP_STRIDE + CMP_LEN - 1
    nb = lp // SEL_BLOCK
    n_sel = min(N_SEL, nb)
    to_blocks = lambda r: jnp.pad(r, pad).reshape(b, nb, SEL_BLOCK, NSA_KV, HEAD_DIM).transpose(0, 3, 1, 2, 4)
    ksb, vsb = to_blocks(ks), to_blocks(vs)
    tab = table[:, :NSA_HEADS].reshape(N_BUCKETS, NSA_KV, NSA_GROUP)
    tab_g = tab.transpose(1, 0, 2)
    bi = jnp.arange(b)[:, None, None, None]
    gi = jnp.arange(NSA_KV)[None, :, None, None]
    blk_ids = jnp.arange(nb)
    scale = HEAD_DIM ** -0.5
    qb = min(Q_BLOCK, t)

    def block(q0):
        qg = lax.dynamic_slice_in_dim(q, q0, qb, 1).reshape(b, qb, NSA_KV, NSA_GROUP, HEAD_DIM)
        gt = lax.dynamic_slice_in_dim(gates, q0, qb, 1).reshape(b, qb, NSA_KV, NSA_GROUP, 3)
        qpos = q_start + q0 + jnp.arange(qb)
        rel = qpos[:, None] - c_end[None, :]
        lg = jnp.einsum('btgnd,bcgd->bgntc', qg, kc).astype(jnp.float32) * scale + tab[t5_bucket(rel)].transpose(2, 3, 0, 1)
        pc = masked_softmax(lg, rel >= 0)
        o_cmp = jnp.einsum('bgntc,bcgd->btgnd', pc.astype(vc.dtype), vc)
        p = jnp.sum(pc, axis=2)
        pp = jnp.pad(p, ((0, 0), (0, 0), (0, 0), (1, CMP_PER_SEL)))
        score = pp[..., :CMP_PER_SEL * nb].reshape(b, NSA_KV, qb, nb, CMP_PER_SEL).sum(-1) + pp[..., CMP_PER_SEL::CMP_PER_SEL]
        forced = (blk_ids[None, :] == (qpos // SEL_BLOCK)[:, None]) | (blk_ids[None, :] == 0)
        valid = blk_ids[None, :] * SEL_BLOCK <= qpos[:, None]
        score = jnp.where(forced, 1e9, jnp.where(valid, score, -1.0))
        _, idx = lax.top_k(score, n_sel)
        ksel = ksb[bi, gi, idx].reshape(b, NSA_KV, qb, n_sel * SEL_BLOCK, HEAD_DIM)
        vsel = vsb[bi, gi, idx].reshape(b, NSA_KV, qb, n_sel * SEL_BLOCK, HEAD_DIM)
        pos = (idx[..., None] * SEL_BLOCK + jnp.arange(SEL_BLOCK)).reshape(b, NSA_KV, qb, n_sel * SEL_BLOCK)
        rel = qpos[:, None] - pos
        bias = tab_g[gi, t5_bucket(rel)].transpose(0, 1, 4, 2, 3)
        ls = jnp.einsum('btgnd,bgtkd->bgntk', qg, ksel).astype(jnp.float32) * scale + bias
        ps = masked_softmax(ls, (rel >= 0)[:, :, None])
        o_sel = jnp.einsum('bgntk,bgtkd->btgnd', ps.astype(vsel.dtype), vsel)
        kw = lax.dynamic_slice_in_dim(kw_all, q0, WINDOW + qb, 1)
        vw = lax.dynamic_slice_in_dim(vw_all, q0, WINDOW + qb, 1)
        wpos = q_start - WINDOW + q0 + jnp.arange(WINDOW + qb)
        rel = qpos[:, None] - wpos[None, :]
        mask = (rel >= 0) & (rel < WINDOW) & (wpos >= 0)[None, :]
        lw = jnp.einsum('btgnd,bwgd->bgntw', qg, kw).astype(jnp.float32) * scale + tab[t5_bucket(rel)].transpose(2, 3, 0, 1)
        pw = masked_softmax(lw, mask)
        o_win = jnp.einsum('bgntw,bwgd->btgnd', pw.astype(vw.dtype), vw)
        o = gt[..., 0:1] * o_cmp + gt[..., 1:2] * o_sel + gt[..., 2:3] * o_win
        return o.reshape(b, qb, NSA_HEADS * HEAD_DIM)

    return sweep_queries(block, t, qb)


def diff_mixer(q, k, v, q_start, table, lam, subln_w, lam_init):
    b, t = q.shape[:2]
    l = k.shape[1]
    q1, q2 = q[..., :DIFF_QK], q[..., DIFF_QK:]
    k1, k2 = k[..., :DIFF_QK], k[..., DIFF_QK:]
    tab = table[:, NSA_HEADS:]
    kpos = jnp.arange(l)
    scale = DIFF_QK ** -0.5
    qb = min(Q_BLOCK, t)

    def block(q0):
        a1 = lax.dynamic_slice_in_dim(q1, q0, qb, 1)
        a2 = lax.dynamic_slice_in_dim(q2, q0, qb, 1)
        qpos = q_start + q0 + jnp.arange(qb)
        rel = qpos[:, None] - kpos[None, :]
        bias = tab[t5_bucket(rel)].transpose(2, 0, 1)
        mask = rel >= 0
        p1 = masked_softmax(jnp.einsum('bthd,bshd->bhts', a1, k1).astype(jnp.float32) * scale + bias, mask)
        p2 = masked_softmax(jnp.einsum('bthd,bshd->bhts', a2, k2).astype(jnp.float32) * scale + bias, mask)
        a = (p1 - lam * p2).astype(v.dtype)
        o = jnp.einsum('bhts,bshd->bthd', a, v).astype(jnp.float32)
        o = o * lax.rsqrt(jnp.mean(jnp.square(o), -1, keepdims=True) + LN_EPS) * subln_w.astype(jnp.float32) * (1.0 - lam_init)
        return o.reshape(b, qb, DIFF_HEADS * HEAD_DIM).astype(v.dtype)

    return sweep_queries(block, t, qb)


def fox_mixer(q, k, v, cum_q, cum_k, q_start):
    b, t = q.shape[:2]
    l = k.shape[1]
    kpos = jnp.arange(l)
    cum_kT = jnp.swapaxes(cum_k, 1, 2)
    scale = HEAD_DIM ** -0.5
    qb = min(Q_BLOCK, t)

    def block(q0):
        a = lax.dynamic_slice_in_dim(q, q0, qb, 1)
        cq = jnp.swapaxes(lax.dynamic_slice_in_dim(cum_q, q0, qb, 1), 1, 2)
        qpos = q_start + q0 + jnp.arange(qb)
        lg = jnp.einsum('bthd,bshd->bhts', a, k).astype(jnp.float32) * scale + cq[..., None] - cum_kT[:, :, None, :]
        p = masked_softmax(lg, qpos[:, None] >= kpos[None, :])
        o = jnp.einsum('bhts,bshd->bthd', p.astype(v.dtype), v)
        return o.reshape(b, qb, FOX_HEADS * HEAD_DIM)

    return sweep_queries(block, t, qb)


def even_project(x, w_in):
    b, t, _ = x.shape
    nq, ck, cv, sk, sv, wk, wv, g, dq, dk, dv = split_cols(x @ w_in, EVEN_WIDTHS)
    kv = lambda r: r.reshape(b, t, NSA_KV, HEAD_DIM)
    hd = lambda r, h: r.reshape(b, t, h, HEAD_DIM)
    nsa_rows = jnp.stack([kv(ck), kv(cv), kv(sk), kv(sv)], axis=2)
    win_rows = jnp.stack([kv(wk), kv(wv)], axis=2)
    diff_rows = jnp.stack([hd(dk, DIFF_HEADS), hd(dv, DIFF_HEADS)], axis=2)
    gates = jax.nn.sigmoid(g.astype(jnp.float32)).reshape(b, t, NSA_HEADS, 3).astype(x.dtype)
    return hd(nq, NSA_HEADS), gates, hd(dq, DIFF_HEADS), nsa_rows, win_rows, diff_rows


def even_attend(nq, gates, dq, nsa_all, win_all, diff_all, q_start, cmp_pos, cmp_w, table, lam, subln_w, lam_init):
    o_nsa = nsa_mixer(nq, gates, q_start, nsa_all[:, :, 0], nsa_all[:, :, 1], nsa_all[:, :, 2], nsa_all[:, :, 3],
                      win_all[:, :, 0], win_all[:, :, 1], cmp_pos, cmp_w, table)
    o_diff = diff_mixer(dq, diff_all[:, :, 0], diff_all[:, :, 1], q_start, table, lam, subln_w, lam_init)
    return jnp.concatenate([o_nsa, o_diff], axis=-1)


def even_mixer_prompt(x, w_in, w_out, cmp_pos, cmp_w, table, lam, subln_w, lam_init):
    nq, gates, dq, nsa_rows, win_rows, diff_rows = even_project(x, w_in)
    win_all = jnp.pad(win_rows, ((0, 0), (WINDOW, 0), (0, 0), (0, 0), (0, 0)))
    o = even_attend(nq, gates, dq, nsa_rows, win_all, diff_rows, 0, cmp_pos, cmp_w, table, lam, subln_w, lam_init)
    return o @ w_out, nsa_rows, win_rows[:, -min(WINDOW, x.shape[1]):], diff_rows


def even_mixer_sample(x, nsa_pool, diff_pool, win_buf, layer_idx, page_table, w_in, w_out, cmp_pos, cmp_w, table, lam, subln_w, lam_init):
    nq, gates, dq, nsa_rows, win_rows, diff_rows = even_project(x, w_in)
    past_len = page_table.shape[1] * PAGE_SIZE
    wb = win_buf.shape[1]

    def one(args):
        pages, buf, q_a, g_a, q_b, r_nsa, r_win, r_diff = args
        nsa_all = jnp.concatenate([gather_pages(nsa_pool, layer_idx, pages), r_nsa], 0)[None]
        diff_all = jnp.concatenate([gather_pages(diff_pool, layer_idx, pages), r_diff], 0)[None]
        win_cat = jnp.concatenate([buf, r_win], 0)
        win_all = jnp.pad(win_cat, ((WINDOW - wb, 0), (0, 0), (0, 0), (0, 0)))[None]
        o = even_attend(q_a[None], g_a[None], q_b[None], nsa_all, win_all, diff_all, past_len,
                        cmp_pos, cmp_w, table, lam, subln_w, lam_init)
        return o[0], win_cat[-wb:]

    o, new_win = lax.map(one, (page_table, win_buf, nq, gates, dq, nsa_rows, win_rows, diff_rows))
    return o @ w_out, nsa_rows, new_win, diff_rows


def odd_project(x, w_in, b_f):
    b, t, _ = x.shape
    q, k, v, f = split_cols(x @ w_in, ODD_WIDTHS)
    hd = lambda r: r.reshape(b, t, FOX_HEADS, HEAD_DIM)
    logf = jax.nn.log_sigmoid(f.astype(jnp.float32) + b_f.astype(jnp.float32))
    return hd(q), jnp.stack([hd(k), hd(v)], axis=2), logf


def odd_mixer_prompt(x, w_in, b_f, w_out):
    q, kv, logf = odd_project(x, w_in, b_f)
    cum = jnp.cumsum(logf, axis=1)
    o = fox_mixer(q, kv[:, :, 0], kv[:, :, 1], cum, cum, 0)
    return o @ w_out, kv, logf


def odd_mixer_sample(x, kv_pool, logf_pool, layer_idx, page_table, w_in, b_f, w_out):
    q, kv, logf = odd_project(x, w_in, b_f)
    past_len = page_table.shape[1] * PAGE_SIZE

    def one(args):
        pages, q_i, kv_i, lf_i = args
        kv_all = jnp.concatenate([gather_pages(kv_pool, layer_idx, pages), kv_i], 0)[None]
        lf_all = jnp.concatenate([gather_pages(logf_pool, layer_idx, pages).astype(jnp.float32), lf_i], 0)
        cum = jnp.cumsum(lf_all, axis=0)[None]
        return fox_mixer(q_i[None], kv_all[:, :, 0], kv_all[:, :, 1], cum[:, past_len:], cum, past_len)[0]

    o = lax.map(one, (page_table, q, kv, logf))
    return o @ w_out, kv, logf


def project_memory(mem, w_kv):
    b = mem.shape[0]
    return (mem @ w_kv).reshape(b, N_MEM, 2, MEM_HEADS, HEAD_DIM)


def memory_attention(h, mem_kv, w_q, w_o):
    b, t, _ = h.shape
    q = (h @ w_q).reshape(b, t, MEM_HEADS, HEAD_DIM)
    lg = jnp.einsum('bthd,bmhd->bhtm', q, mem_kv[:, :, 0]).astype(jnp.float32) * HEAD_DIM ** -0.5
    p = jax.nn.softmax(lg, axis=-1)
    o = jnp.einsum('bhtm,bmhd->bthd', p.astype(h.dtype), mem_kv[:, :, 1])
    return o.reshape(b, t, MEM_WIDTH) @ w_o


def moe_ffn(x, w_router, b_router, w_gate, w_up, w_down, ws_gate, ws_up, ws_down):
    b, t, d = x.shape
    n = b * t
    xf = x.reshape(n, d)
    aff = jax.nn.sigmoid(jnp.dot(xf, w_router).astype(jnp.float32))
    biased = aff + b_router.astype(jnp.float32)
    per_grp = N_EXPERTS // N_EXPERT_GROUPS
    grp_score = lax.top_k(biased.reshape(n, N_EXPERT_GROUPS, per_grp), 2)[0].sum(-1)
    _, grp_idx = lax.top_k(grp_score, TOPK_GROUPS)
    grp_mask = jnp.sum(jax.nn.one_hot(grp_idx, N_EXPERT_GROUPS, dtype=jnp.float32), axis=1) > 0
    expert_mask = jnp.repeat(grp_mask, per_grp, axis=1)
    _, eidx = lax.top_k(jnp.where(expert_mask, biased, NEG_INF), TOP_K)
    gate = jnp.take_along_axis(aff, eidx, axis=1)
    gate = gate / jnp.sum(gate, -1, keepdims=True) * ROUTE_SCALE
    nk = n * TOP_K
    blk = int(min(256, max(8, 2 ** int(math.log2(max(nk // N_EXPERTS, 1))))))
    flat_e = eidx.reshape(-1)
    order = jnp.argsort(flat_e)
    sorted_e = flat_e[order]
    counts = jnp.bincount(flat_e, length=N_EXPERTS)
    padded = (counts + blk - 1) // blk * blk
    pad_end = jnp.cumsum(padded)
    pad_start = pad_end - padded
    start = jnp.cumsum(counts) - counts
    dest = pad_start[sorted_e] + jnp.arange(nk) - start[sorted_e]
    n_rows = -(-(nk + N_EXPERTS * (blk - 1)) // blk) * blk
    row_tok = jnp.full((n_rows,), n, jnp.int32).at[dest].set(order // TOP_K)
    row_gate = jnp.zeros((n_rows,), jnp.float32).at[dest].set(gate.reshape(-1)[order])
    blk_expert = jnp.minimum(jnp.searchsorted(pad_end, jnp.arange(n_rows // blk) * blk, side='right'), N_EXPERTS - 1)
    x_pad = jnp.concatenate([xf, jnp.zeros((1, d), xf.dtype)], axis=0)

    def expert_block(args):
        tok, e = args
        xs = x_pad[tok]
        h = jax.nn.silu(xs @ w_gate[e]) * (xs @ w_up[e])
        return h @ w_down[e]

    ys = lax.map(expert_block, (row_tok.reshape(-1, blk), blk_expert)).reshape(n_rows, d)
    routed = jnp.zeros((n + 1, d), jnp.float32).at[row_tok].add(ys.astype(jnp.float32) * row_gate[:, None])[:n]
    shared = (jax.nn.silu(xf @ ws_gate) * (xf @ ws_up)) @ ws_down
    return (routed.astype(x.dtype) + shared).reshape(b, t, d)


def setup_inputs(seed: int = 0) -> dict:
    key = jax.random.key(seed)
    keys = list(jax.random.split(key, 40))

    def nrm(shape, scale=1.0):
        return jax.random.normal(keys.pop(), shape, jnp.float32) * scale

    d = D_MODEL
    n_pages = PAST_LEN // PAGE_SIZE
    n_used = DEC_BATCH * n_pages
    n_pool = (5 * n_used + 3) // 4
    wb = min(WINDOW, PAST_LEN)
    mix_even = (NSA_HEADS + DIFF_HEADS) * HEAD_DIM
    mix_odd = FOX_HEADS * HEAD_DIM
    page_table = jax.random.permutation(keys.pop(), n_pool)[:n_used].reshape(DEC_BATCH, n_pages).astype(jnp.int32)
    return {
        'x_prompt': nrm((BATCH, SEQ, d)),
        'x_sample': nrm((DEC_BATCH, DEC_SEQ, d)),
        'cache_nsa_kv': nrm((N_EVEN, n_pool, PAGE_SIZE, 4, NSA_KV, HEAD_DIM)),
        'cache_nsa_win': nrm((N_EVEN, DEC_BATCH, wb, 2, NSA_KV, HEAD_DIM)),
        'cache_diff_kv': nrm((N_EVEN, n_pool, PAGE_SIZE, 2, DIFF_HEADS, HEAD_DIM)),
        'cache_fox_kv': nrm((N_ODD, n_pool, PAGE_SIZE, 2, FOX_HEADS, HEAD_DIM)),
        'cache_fox_logf': jax.nn.log_sigmoid(FORGET_BIAS_MEAN + nrm((N_ODD, n_pool, PAGE_SIZE, FOX_HEADS), 0.5)),
        'cache_mem_kv': nrm((DEPTH, DEC_BATCH, N_MEM, 2, MEM_HEADS, HEAD_DIM)),
        'page_table': page_table,
        'mem_prompt': nrm((BATCH, N_MEM, d)),
        'rel_bias': nrm((N_BUCKETS, NSA_HEADS + DIFF_HEADS), 0.2),
        'w_even_in': nrm((N_EVEN, d, sum(EVEN_WIDTHS)), d ** -0.5),
        'w_even_out': nrm((N_EVEN, mix_even, d), mix_even ** -0.5 * DEEPNORM_BETA),
        'nsa_cmp_pos': nrm((N_EVEN, 2, CMP_LEN, NSA_KV, HEAD_DIM), 0.1),
        'nsa_cmp_w': nrm((N_EVEN, 2, NSA_KV, CMP_LEN, HEAD_DIM, HEAD_DIM), (CMP_LEN * HEAD_DIM) ** -0.5),
        'diff_lambda': nrm((N_EVEN, 4, DIFF_QK), 0.1),
        'diff_subln': 1.0 + nrm((N_EVEN, HEAD_DIM), 0.02),
        'w_odd_in': nrm((N_ODD, d, sum(ODD_WIDTHS)), d ** -0.5),
        'b_forget': FORGET_BIAS_MEAN + nrm((N_ODD, FOX_HEADS), 0.5),
        'w_odd_out': nrm((N_ODD, mix_odd, d), mix_odd ** -0.5 * DEEPNORM_BETA),
        'w_mem_q': nrm((DEPTH, d, MEM_WIDTH), d ** -0.5),
        'w_mem_kv': nrm((DEPTH, d, 2 * MEM_WIDTH), d ** -0.5),
        'w_mem_o': nrm((DEPTH, MEM_WIDTH, d), MEM_WIDTH ** -0.5 * DEEPNORM_BETA),
        'ln_g': 1.0 + nrm((DEPTH, 3, d), 0.02),
        'ln_b': nrm((DEPTH, 3, d), 0.02),
        'w_router': nrm((DEPTH, d, N_EXPERTS), d ** -0.5),
        'b_router': nrm((DEPTH, N_EXPERTS), 0.01),
        'w_exp_gate': nrm((DEPTH, N_EXPERTS, d, D_EXPERT), d ** -0.5),
        'w_exp_up': nrm((DEPTH, N_EXPERTS, d, D_EXPERT), d ** -0.5),
        'w_exp_down': nrm((DEPTH, N_EXPERTS, D_EXPERT, d), D_EXPERT ** -0.5 * DEEPNORM_BETA),
        'w_sh_gate': nrm((DEPTH, d, D_EXPERT), d ** -0.5),
        'w_sh_up': nrm((DEPTH, d, D_EXPERT), d ** -0.5),
        'w_sh_down': nrm((DEPTH, D_EXPERT, d), D_EXPERT ** -0.5 * DEEPNORM_BETA),
    }


def reference(x_prompt, x_sample, cache_nsa_kv, cache_nsa_win, cache_diff_kv, cache_fox_kv, cache_fox_logf,
              cache_mem_kv, page_table, mem_prompt, rel_bias, w_even_in, w_even_out, nsa_cmp_pos, nsa_cmp_w,
              diff_lambda, diff_subln, w_odd_in, b_forget, w_odd_out, w_mem_q, w_mem_kv, w_mem_o, ln_g, ln_b,
              w_router, b_router, w_exp_gate, w_exp_up, w_exp_down, w_sh_gate, w_sh_up, w_sh_down):
    hp, hs = x_prompt, x_sample
    p_nsa, p_win, p_diff, p_fox, p_logf, p_mem = [], [], [], [], [], []
    s_nsa, s_win, s_diff, s_fox, s_logf = [], [], [], [], []
    for layer in range(DEPTH):
        i = layer // 2
        if layer % 2 == 0:
            lam_init = 0.8 - 0.6 * math.exp(-0.3 * layer)
            dl = diff_lambda[i].astype(jnp.float32)
            lam = jnp.exp(jnp.sum(dl[0] * dl[1])) - jnp.exp(jnp.sum(dl[2] * dl[3])) + lam_init
            mp, r_nsa, r_win, r_diff = even_mixer_prompt(hp, w_even_in[i], w_even_out[i], nsa_cmp_pos[i], nsa_cmp_w[i],
                                                         rel_bias, lam, diff_subln[i], lam_init)
            ms, q_nsa, q_win, q_diff = even_mixer_sample(hs, cache_nsa_kv, cache_diff_kv, cache_nsa_win[i], i, page_table,
                                                         w_even_in[i], w_even_out[i], nsa_cmp_pos[i], nsa_cmp_w[i],
                                                         rel_bias, lam, diff_subln[i], lam_init)
            p_nsa.append(r_nsa)
            p_win.append(r_win)
            p_diff.append(r_diff)
            s_nsa.append(q_nsa)
            s_win.append(q_win)
            s_diff.append(q_diff)
        else:
            mp, r_kv, r_lf = odd_mixer_prompt(hp, w_odd_in[i], b_forget[i], w_odd_out[i])
            ms, q_kv, q_lf = odd_mixer_sample(hs, cache_fox_kv, cache_fox_logf, i, page_table,
                                              w_odd_in[i], b_forget[i], w_odd_out[i])
            p_fox.append(r_kv)
            p_logf.append(r_lf)
            s_fox.append(q_kv)
            s_logf.append(q_lf)
        hp = layer_norm(DEEPNORM_ALPHA * hp + mp, ln_g[layer, 0], ln_b[layer, 0])
        hs = layer_norm(DEEPNORM_ALPHA * hs + ms, ln_g[layer, 0], ln_b[layer, 0])
        mem_kv = project_memory(mem_prompt, w_mem_kv[layer])
        p_mem.append(mem_kv)
        hp = layer_norm(DEEPNORM_ALPHA * hp + memory_attention(hp, mem_kv, w_mem_q[layer], w_mem_o[layer]),
                        ln_g[layer, 1], ln_b[layer, 1])
        hs = layer_norm(DEEPNORM_ALPHA * hs + memory_attention(hs, cache_mem_kv[layer], w_mem_q[layer], w_mem_o[layer]),
                        ln_g[layer, 1], ln_b[layer, 1])
        moe_w = (w_router[layer], b_router[layer], w_exp_gate[layer], w_exp_up[layer], w_exp_down[layer],
                 w_sh_gate[layer], w_sh_up[layer], w_sh_down[layer])
        hp = layer_norm(DEEPNORM_ALPHA * hp + moe_ffn(hp, *moe_w), ln_g[layer, 2], ln_b[layer, 2])
        hs = layer_norm(DEEPNORM_ALPHA * hs + moe_ffn(hs, *moe_w), ln_g[layer, 2], ln_b[layer, 2])
    return (hp, hs, jnp.stack(p_nsa), jnp.stack(p_win), jnp.stack(p_diff), jnp.stack(p_fox), jnp.stack(p_logf),
            jnp.stack(p_mem), jnp.stack(s_nsa), jnp.stack(s_win), jnp.stack(s_diff), jnp.stack(s_fox), jnp.stack(s_logf))
```

```python
import functools
import math

import numpy as np
import jax
import jax.numpy as jnp
from jax import lax
from jax.experimental import pallas as pl
from jax.experimental.pallas import tpu as pltpu

F32 = jnp.float32
_MM = jnp.bfloat16

D_MODEL = 2048
HEAD_DIM = 128
NSA_HEADS = 8
NSA_KV = 2
NSA_GROUP = NSA_HEADS // NSA_KV
CMP_STRIDE = 16
CMP_LEN = 2 * CMP_STRIDE
SEL_BLOCK = 64
CMP_PER_SEL = SEL_BLOCK // CMP_STRIDE
N_SEL = 16
WINDOW = 512
DIFF_HEADS = 8
DIFF_QK = HEAD_DIM // 2
FOX_HEADS = 16
N_MEM = 256
MEM_HEADS = 4
MEM_WIDTH = MEM_HEADS * HEAD_DIM
N_BUCKETS = 32
MAX_DISTANCE = 128
N_EXPERTS = 64
N_EXPERT_GROUPS = 8
TOPK_GROUPS = 4
TOP_K = 8
D_EXPERT = 512
ROUTE_SCALE = 2.5
PAGE_SIZE = 128
DEPTH = 2
DEEPNORM_ALPHA = (2 * DEPTH) ** 0.25
LN_EPS = 1e-5
MASKED = -1e30
SCALE = HEAD_DIM ** -0.5
DIFF_SCALE = DIFF_QK ** -0.5
LANES = 128
VMEM_LIMIT = 56 * 1024 * 1024
MOE_BLK = 256

E_NQ, E_CK, E_CV, E_SK, E_SV, E_WK, E_WV, E_DQ, E_DK, E_DV, E_G = 0, 8, 10, 12, 14, 16, 18, 20, 28, 36, 44
E_COLS = 45 * LANES
O_Q, O_K, O_V, O_F = 0, 16, 32, 48
O_COLS = 49 * LANES


def _cparams(n_grid):
    return pltpu.CompilerParams(dimension_semantics=("arbitrary",) * n_grid, vmem_limit_bytes=VMEM_LIMIT)


def _mmd(a, b):
    return jnp.dot(a.astype(_MM), b.astype(_MM), preferred_element_type=F32)


def _mmd_nt(a, b):
    return lax.dot_general(a.astype(_MM), b.astype(_MM), (((1,), (1,)), ((), ())), preferred_element_type=F32)


def _split3(x):
    hi = x.astype(_MM)
    r = x - hi.astype(F32)
    mid = r.astype(_MM)
    lo = (r - mid.astype(F32)).astype(_MM)
    return hi, mid, lo


def _mmd_nt_precise(a, b):
    nt = lambda x, y: lax.dot_general(x, y, (((1,), (1,)), ((), ())), preferred_element_type=F32)
    a0, a1, a2 = _split3(a)
    b0, b1, b2 = _split3(b)
    return ((nt(a2, b0) + nt(a0, b2) + nt(a1, b1)) + (nt(a1, b0) + nt(a0, b1))) + nt(a0, b0)


def _mmd_exact_rhs01(x, m01):
    hi, mid, lo = _split3(x)
    f = lambda a: jnp.dot(a, m01, preferred_element_type=F32)
    return f(hi) + f(mid) + f(lo)


def _mmd_exact_lhs01(m01, x):
    hi, mid, lo = _split3(x)
    f = lambda a: jnp.dot(m01, a, preferred_element_type=F32)
    return f(hi) + f(mid) + f(lo)


def _t5_bias(tbl, rel):
    idx = jnp.clip(rel, 0, LANES - 1)
    parts = [jnp.take_along_axis(tbl, idx[:, c * LANES:(c + 1) * LANES], axis=1) for c in range(rel.shape[1] // LANES)]
    return parts[0] if len(parts) == 1 else jnp.concatenate(parts, axis=1)


def _online_update(carry, s, v):
    m, l, acc = carry
    m_new = jnp.maximum(m, jnp.max(s, axis=-1, keepdims=True))
    alpha = jnp.exp(m - m_new)
    p = jnp.exp(s - m_new)
    l_new = alpha * l + jnp.sum(p, axis=-1, keepdims=True)
    acc_new = alpha * acc + _mmd(p, v)
    return m_new, l_new, acc_new


def _flash_init(rows, width):
    return (jnp.full((rows, 1), MASKED, F32), jnp.zeros((rows, 1), F32), jnp.zeros((rows, width), F32))


def _layer_norm(y, g, b):
    mu = jnp.mean(y, axis=-1, keepdims=True)
    yc = y - mu
    var = jnp.mean(yc * yc, axis=-1, keepdims=True)
    return yc * lax.rsqrt(var + LN_EPS) * g + b


def _mm_kernel(x_ref, w_ref, o_ref):
    o_ref[...] = _mmd(x_ref[...], w_ref[...])


def _mm(x, w, tm, tn):
    m, k = x.shape
    n = w.shape[1]
    return pl.pallas_call(
        _mm_kernel, grid=(n // tn, m // tm),
        in_specs=[pl.BlockSpec((tm, k), lambda j, i: (i, 0)), pl.BlockSpec((k, tn), lambda j, i: (0, j))],
        out_specs=pl.BlockSpec((tm, tn), lambda j, i: (i, j)),
        out_shape=jax.ShapeDtypeStruct((m, n), F32), compiler_params=_cparams(2), name="mm")(x, w)


def _mm_ln_kernel(*refs, n_in):
    xs, ws = refs[:n_in], refs[n_in:2 * n_in]
    res_ref, g_ref, b_ref, o_ref = refs[2 * n_in:]
    acc = _mmd(xs[0][...], ws[0][...])
    for x_ref, w_ref in zip(xs[1:], ws[1:]):
        acc = acc + _mmd(x_ref[...], w_ref[...])
    o_ref[...] = _layer_norm(DEEPNORM_ALPHA * res_ref[...] + acc, g_ref[...], b_ref[...])


def _mm_ln(xs, ws, res, g, b, tm):
    m, d = res.shape
    n_in = len(xs)
    in_specs = [pl.BlockSpec((tm, x.shape[1]), lambda i: (i, 0)) for x in xs]
    in_specs += [pl.BlockSpec(w.shape, lambda i: (0, 0)) for w in ws]
    in_specs += [pl.BlockSpec((tm, d), lambda i: (i, 0)), pl.BlockSpec((1, d), lambda i: (0, 0)),
                 pl.BlockSpec((1, d), lambda i: (0, 0))]
    return pl.pallas_call(
        functools.partial(_mm_ln_kernel, n_in=n_in), grid=(m // tm,), in_specs=in_specs,
        out_specs=pl.BlockSpec((tm, d), lambda i: (i, 0)), out_shape=jax.ShapeDtypeStruct((m, d), F32),
        compiler_params=_cparams(1), name="mm_ln")(*xs, *ws, res, g.reshape(1, d), b.reshape(1, d))


def _compress(h, wcat, pos2):
    n = h.shape[0]
    ab = _mmd(h, wcat)
    r = _mmd(pos2, wcat)
    const = r[0:1, :HEAD_DIM] + r[1:2, HEAD_DIM:]
    b_next = pltpu.roll(ab[:, HEAD_DIM:], n - 1, 0)
    return ab[:, :HEAD_DIM] + b_next + const


def _compress_kernel(h_ref, w_ref, p_ref, o_ref):
    o_ref[0] = _compress(h_ref[0], w_ref[0], p_ref[0])


def _compress_prompt(h4, wcat, pos2):
    n = h4.shape[1]
    return pl.pallas_call(
        _compress_kernel, grid=(4,),
        in_specs=[pl.BlockSpec((1, n, 16 * HEAD_DIM), lambda s: (s, 0, 0)),
                  pl.BlockSpec((1, 16 * HEAD_DIM, 2 * HEAD_DIM), lambda s: (s, 0, 0)),
                  pl.BlockSpec((1, 8, 16 * HEAD_DIM), lambda s: (s, 0, 0))],
        out_specs=pl.BlockSpec((1, n, HEAD_DIM), lambda s: (s, 0, 0)),
        out_shape=jax.ShapeDtypeStruct((4, n, HEAD_DIM), F32), compiler_params=_cparams(1),
        name="nsa_compress")(h4, wcat, pos2)


def _sel_score_matrix(n_cmp_pad, n_blk):
    c = lax.broadcasted_iota(jnp.int32, (n_cmp_pad, n_blk), 0)
    b = lax.broadcasted_iota(jnp.int32, (n_cmp_pad, n_blk), 1)
    return ((c >= CMP_PER_SEL * b - 1) & (c <= CMP_PER_SEL * b + CMP_PER_SEL - 1)).astype(_MM)


def _top_blocks(score, n_pick):
    blk = lax.broadcasted_iota(jnp.int32, score.shape, 1)
    sel = jnp.zeros(score.shape, F32)
    for _ in range(n_pick):
        m = jnp.max(score, axis=-1, keepdims=True)
        first = jnp.min(jnp.where(score == m, blk, score.shape[1]), axis=-1, keepdims=True)
        pick = blk == first
        sel = jnp.where(pick, 1.0, sel)
        score = jnp.where(pick, -jnp.inf, score)
    return sel


def _nsa_cmp_kernel(f_ref, q_ref, kc_ref, vc_ref, o_ref, sel_ref, *, tq, n_cmp):
    i = pl.program_id(1)
    n_pad = kc_ref.shape[1]
    n_blk = sel_ref.shape[2]
    kc = kc_ref[0].astype(_MM)
    vc = vc_ref[0].astype(_MM)
    qpos_c = i * tq + lax.broadcasted_iota(jnp.int32, (tq, n_pad), 0)
    c = lax.broadcasted_iota(jnp.int32, (tq, n_pad), 1)
    rel = qpos_c - (c * CMP_STRIDE + CMP_LEN - 1)
    vis = (rel >= 0) & (c < n_cmp)
    psum = jnp.zeros((tq, n_pad), F32)
    for n in range(NSA_GROUP):
        qh = q_ref[:, n * HEAD_DIM:(n + 1) * HEAD_DIM] * SCALE
        tbl = jnp.broadcast_to(f_ref[0, n:n + 1, :], (tq, LANES))
        lg = jnp.where(vis, _mmd_nt(qh, kc) + _t5_bias(tbl, rel), MASKED)
        m = jnp.max(lg, axis=-1, keepdims=True)
        e = jnp.where(vis, jnp.exp(lg - m), 0.0)
        p = e / jnp.maximum(jnp.sum(e, axis=-1, keepdims=True), 1e-30)
        o_ref[:, n * HEAD_DIM:(n + 1) * HEAD_DIM] = _mmd(p, vc)
        psum = psum + p
    score = _mmd_exact_rhs01(psum, _sel_score_matrix(n_pad, n_blk))
    blk = lax.broadcasted_iota(jnp.int32, (tq, n_blk), 1)
    qpos = i * tq + lax.broadcasted_iota(jnp.int32, (tq, n_blk), 0)
    forced = (blk == qpos // SEL_BLOCK) | (blk == 0)
    valid = blk * SEL_BLOCK <= qpos
    score = jnp.where(forced, 1e9, jnp.where(valid, score, -1.0))
    sel_ref[0] = _top_blocks(score, min(N_SEL, n_blk))


def _nsa_cmp_prompt(p0, kvc, ftab, t, tq):
    n_pad = kvc.shape[1]
    n_blk = t // SEL_BLOCK
    return pl.pallas_call(
        functools.partial(_nsa_cmp_kernel, tq=tq, n_cmp=t // CMP_STRIDE - 1), grid=(NSA_KV, t // tq),
        in_specs=[pl.BlockSpec((1, NSA_GROUP, LANES), lambda g, i: (g, 0, 0)),
                  pl.BlockSpec((tq, NSA_GROUP * HEAD_DIM), lambda g, i: (i, g)),
                  pl.BlockSpec((1, n_pad, HEAD_DIM), lambda g, i: (g, 0, 0)),
                  pl.BlockSpec((1, n_pad, HEAD_DIM), lambda g, i: (NSA_KV + g, 0, 0))],
        out_specs=[pl.BlockSpec((tq, NSA_GROUP * HEAD_DIM), lambda g, i: (i, g)),
                   pl.BlockSpec((1, tq, n_blk), lambda g, i: (g, i, 0))],
        out_shape=[jax.ShapeDtypeStruct((t, NSA_HEADS * HEAD_DIM), F32),
                   jax.ShapeDtypeStruct((NSA_KV, t, n_blk), F32)],
        compiler_params=_cparams(2), name="nsa_cmp")(ftab, p0, kvc, kvc)


def _nsa_sel_kernel(f_ref, q_ref, k_ref, v_ref, sel_ref, o_ref, *, tq):
    i = pl.program_id(1)
    q = q_ref[...] * SCALE
    selm = sel_ref[0].astype(_MM)
    n_blk = selm.shape[1]
    tbl = jnp.broadcast_to(f_ref[0], (tq, LANES))
    far_bias = f_ref[0][:, LANES - 1:LANES]
    blk_r = lax.broadcasted_iota(jnp.int32, (n_blk, tq), 0)
    blk_c = lax.broadcasted_iota(jnp.int32, (n_blk, tq), 1) // SEL_BLOCK
    row = lax.broadcasted_iota(jnp.int32, (tq, tq), 0)
    col = lax.broadcasted_iota(jnp.int32, (tq, tq), 1)

    def tile(j, carry, near, on=True):
        off = pl.multiple_of(j * tq, tq)
        s = _mmd_nt(q, k_ref[pl.ds(off, tq), :])
        expand = (blk_r == blk_c + j * (tq // SEL_BLOCK)).astype(_MM)
        picked = jnp.dot(selm, expand, preferred_element_type=F32) > 0.5
        if near:
            rel = (i - j) * tq + row - col
            s = jnp.where(picked & (rel >= 0) & on, s + _t5_bias(tbl, rel), -jnp.inf)
        else:
            s = jnp.where(picked, s + far_bias, -jnp.inf)
        return _online_update(carry, s, v_ref[pl.ds(off, tq), :])

    carry = lax.fori_loop(0, jnp.maximum(i - 1, 0), lambda j, c: tile(j, c, False), _flash_init(tq, HEAD_DIM))
    carry = tile(jnp.maximum(i - 1, 0), carry, True, i >= 1)
    _, l, acc = tile(i, carry, True)
    o_ref[...] = acc / l


def _nsa_sel_prompt(p0, selmask, ftab, t, tq):
    n_blk = selmask.shape[2]
    return pl.pallas_call(
        functools.partial(_nsa_sel_kernel, tq=tq), grid=(NSA_HEADS, t // tq),
        in_specs=[pl.BlockSpec((1, 1, LANES), lambda h, i: (h, 0, 0)),
                  pl.BlockSpec((tq, HEAD_DIM), lambda h, i: (i, E_NQ + h)),
                  pl.BlockSpec((t, HEAD_DIM), lambda h, i: (0, E_SK + h // NSA_GROUP)),
                  pl.BlockSpec((t, HEAD_DIM), lambda h, i: (0, E_SV + h // NSA_GROUP)),
                  pl.BlockSpec((1, tq, n_blk), lambda h, i: (h // NSA_GROUP, i, 0))],
        out_specs=pl.BlockSpec((tq, HEAD_DIM), lambda h, i: (i, h)),
        out_shape=jax.ShapeDtypeStruct((t, NSA_HEADS * HEAD_DIM), F32),
        compiler_params=_cparams(2), name="nsa_sel")(ftab, p0, p0, p0, selmask)


def _nsa_win_kernel(f_ref, q_ref, kp_ref, kc_ref, vp_ref, vc_ref, g_ref, ocmp_ref, osel_ref, o_ref, *, tq):
    h = pl.program_id(0)
    i = pl.program_id(1)
    q = q_ref[...] * SCALE
    tbl = jnp.broadcast_to(f_ref[0], (tq, LANES))
    row = lax.broadcasted_iota(jnp.int32, (tq, tq), 0)
    col = lax.broadcasted_iota(jnp.int32, (tq, tq), 1)
    rel_prev = tq + row - col
    s = _mmd_nt(q, kp_ref[...])
    s = jnp.where((rel_prev < WINDOW) & (i >= 1), s + _t5_bias(tbl, rel_prev), -jnp.inf)
    carry = _online_update(_flash_init(tq, HEAD_DIM), s, vp_ref[...])
    rel = row - col
    s = _mmd_nt(q, kc_ref[...])
    s = jnp.where((rel >= 0) & (rel < WINDOW), s + _t5_bias(tbl, rel), -jnp.inf)
    _, l, acc = _online_update(carry, s, vc_ref[...])
    gates = jax.nn.sigmoid(g_ref[...])
    lane = lax.broadcasted_iota(jnp.int32, (tq, LANES), 1)
    gate = lambda k: jnp.sum(jnp.where(lane == 3 * h + k, gates, 0.0), axis=-1, keepdims=True)
    o_ref[...] = gate(0) * ocmp_ref[...] + gate(1) * osel_ref[...] + gate(2) * (acc / l)


def _nsa_win_prompt(p0, o_cmp, o_sel, ftab, t, tq):
    assert tq == WINDOW
    prev = lambda i: jnp.maximum(i - 1, 0)
    grp = lambda h: h // NSA_GROUP
    blk = lambda f: pl.BlockSpec((tq, HEAD_DIM), f)
    return pl.pallas_call(
        functools.partial(_nsa_win_kernel, tq=tq), grid=(NSA_HEADS, t // tq),
        in_specs=[pl.BlockSpec((1, 1, LANES), lambda h, i: (h, 0, 0)),
                  blk(lambda h, i: (i, E_NQ + h)),
                  blk(lambda h, i: (prev(i), E_WK + grp(h))), blk(lambda h, i: (i, E_WK + grp(h))),
                  blk(lambda h, i: (prev(i), E_WV + grp(h))), blk(lambda h, i: (i, E_WV + grp(h))),
                  blk(lambda h, i: (i, E_G)), blk(lambda h, i: (i, h)), blk(lambda h, i: (i, h))],
        out_specs=blk(lambda h, i: (i, h)),
        out_shape=jax.ShapeDtypeStruct((t, NSA_HEADS * HEAD_DIM), F32),
        compiler_params=_cparams(2), name="nsa_win")(ftab, p0, p0, p0, p0, p0, p0, o_cmp, o_sel)


def _diff_lambda(dl, lam_init):
    return (jnp.exp(jnp.sum(dl[0:1] * dl[1:2], axis=-1, keepdims=True))
            - jnp.exp(jnp.sum(dl[2:3] * dl[3:4], axis=-1, keepdims=True)) + lam_init)


def _diff_prompt_kernel(f_ref, q_ref, k_ref, v_ref, lam_ref, sub_ref, o_ref, *, tq, lam_init):
    i = pl.program_id(1)
    lane = lax.broadcasted_iota(jnp.int32, (tq, HEAD_DIM), 1)
    qf = q_ref[...] * DIFF_SCALE
    q12 = jnp.concatenate([jnp.where(lane < DIFF_QK, qf, 0.0), jnp.where(lane >= DIFF_QK, qf, 0.0)], axis=0)
    tbl = jnp.broadcast_to(f_ref[0], (tq, LANES))
    far_bias = f_ref[0][:, LANES - 1:LANES]
    row = lax.broadcasted_iota(jnp.int32, (tq, tq), 0)
    col = lax.broadcasted_iota(jnp.int32, (tq, tq), 1)

    def tile(j, carry, near, on=True):
        off = pl.multiple_of(j * tq, tq)
        s = _mmd_nt(q12, k_ref[pl.ds(off, tq), :])
        if near:
            rel = (i - j) * tq + row - col
            bias = jnp.where((rel >= 0) & on, _t5_bias(tbl, rel), -jnp.inf)
            s = s + jnp.concatenate([bias, bias], axis=0)
        else:
            s = s + far_bias
        return _online_update(carry, s, v_ref[pl.ds(off, tq), :])

    carry = lax.fori_loop(0, jnp.maximum(i - 1, 0), lambda j, c: tile(j, c, False), _flash_init(2 * tq, HEAD_DIM))
    carry = tile(jnp.maximum(i - 1, 0), carry, True, i >= 1)
    _, l, acc = tile(i, carry, True)
    o12 = acc / l
    lam = _diff_lambda(lam_ref[...], lam_init)
    o = o12[:tq] - lam * o12[tq:]
    o = o * lax.rsqrt(jnp.mean(o * o, axis=-1, keepdims=True) + LN_EPS) * sub_ref[...] * (1.0 - lam_init)
    o_ref[...] = o


def _diff_prompt(p0, ftab, lam4, subln, t, tq, lam_init):
    return pl.pallas_call(
        functools.partial(_diff_prompt_kernel, tq=tq, lam_init=lam_init), grid=(DIFF_HEADS, t // tq),
        in_specs=[pl.BlockSpec((1, 1, LANES), lambda h, i: (NSA_HEADS + h, 0, 0)),
                  pl.BlockSpec((tq, HEAD_DIM), lambda h, i: (i, E_DQ + h)),
                  pl.BlockSpec((t, HEAD_DIM), lambda h, i: (0, E_DK + h)),
                  pl.BlockSpec((t, HEAD_DIM), lambda h, i: (0, E_DV + h)),
                  pl.BlockSpec((4, DIFF_QK), lambda h, i: (0, 0)),
                  pl.BlockSpec((1, HEAD_DIM), lambda h, i: (0, 0))],
        out_specs=pl.BlockSpec((tq, HEAD_DIM), lambda h, i: (i, h)),
        out_shape=jax.ShapeDtypeStruct((t, DIFF_HEADS * HEAD_DIM), F32),
        compiler_params=_cparams(2), name="diff_prompt")(ftab, p0, p0, p0, lam4, subln)


def _log_sigmoid(x):
    return jnp.minimum(x, 0.0) - jnp.log(1.0 + jnp.exp(-jnp.abs(x)))


def _fox_prep_kernel(f_ref, b_ref, lf_ref, cum_ref, carry_ref, *, tm):
    @pl.when(pl.program_id(0) == 0)
    def _():
        carry_ref[...] = jnp.zeros(carry_ref.shape, F32)

    lf = _log_sigmoid(f_ref[...] + b_ref[...])
    lf_ref[...] = lf
    r = lax.broadcasted_iota(jnp.int32, (tm, tm), 0)
    c = lax.broadcasted_iota(jnp.int32, (tm, tm), 1)
    cum = _mmd_exact_lhs01((r >= c).astype(_MM), lf) + carry_ref[...]
    cum_ref[...] = cum
    carry_ref[...] = cum[tm - 1:tm, :]


def _fox_prep(p1, b_f, tm):
    m = p1.shape[0]
    b = jnp.pad(b_f.reshape(1, FOX_HEADS), ((0, 0), (0, LANES - FOX_HEADS)))
    return pl.pallas_call(
        functools.partial(_fox_prep_kernel, tm=tm), grid=(m // tm,),
        in_specs=[pl.BlockSpec((tm, LANES), lambda i: (i, O_F)), pl.BlockSpec((1, LANES), lambda i: (0, 0))],
        out_specs=[pl.BlockSpec((tm, LANES), lambda i: (i, 0)), pl.BlockSpec((tm, LANES), lambda i: (i, 0))],
        out_shape=[jax.ShapeDtypeStruct((m, LANES), F32), jax.ShapeDtypeStruct((m, LANES), F32)],
        scratch_shapes=[pltpu.VMEM((1, LANES), F32)], compiler_params=_cparams(1), name="fox_prep")(p1, b)


def _fox_prompt_kernel(q_ref, k_ref, v_ref, cum_ref, ck_ref, o_ref, *, tq):
    h = pl.program_id(0)
    i = pl.program_id(1)
    q = q_ref[...] * SCALE
    lane = lax.broadcasted_iota(jnp.int32, (tq, LANES), 1)
    cq = jnp.sum(jnp.where(lane == h, cum_ref[...], 0.0), axis=-1, keepdims=True)
    row = lax.broadcasted_iota(jnp.int32, (tq, tq), 0)
    col = lax.broadcasted_iota(jnp.int32, (tq, tq), 1)

    def tile(j, carry, diag):
        off = pl.multiple_of(j * tq, tq)
        s = (_mmd_nt(q, k_ref[pl.ds(off, tq), :]) + cq) - ck_ref[0, j]
        if diag:
            s = jnp.where(row >= col, s, -jnp.inf)
        return _online_update(carry, s, v_ref[pl.ds(off, tq), :])

    carry = lax.fori_loop(0, i, lambda j, c: tile(j, c, False), _flash_init(tq, HEAD_DIM))
    _, l, acc = tile(i, carry, True)
    o_ref[...] = acc / l


def _fox_prompt(p1, cum, t, tq):
    ck = cum[:, :FOX_HEADS].T.reshape(FOX_HEADS, t // tq, 1, tq)
    return pl.pallas_call(
        functools.partial(_fox_prompt_kernel, tq=tq), grid=(FOX_HEADS, t // tq),
        in_specs=[pl.BlockSpec((tq, HEAD_DIM), lambda h, i: (i, O_Q + h)),
                  pl.BlockSpec((t, HEAD_DIM), lambda h, i: (0, O_K + h)),
                  pl.BlockSpec((t, HEAD_DIM), lambda h, i: (0, O_V + h)),
                  pl.BlockSpec((tq, LANES), lambda h, i: (i, 0)),
                  pl.BlockSpec((1, t // tq, 1, tq), lambda h, i: (h, 0, 0, 0))],
        out_specs=pl.BlockSpec((tq, HEAD_DIM), lambda h, i: (i, h)),
        out_shape=jax.ShapeDtypeStruct((t, FOX_HEADS * HEAD_DIM), F32),
        compiler_params=_cparams(2), name="fox_prompt")(p1, p1, p1, cum, ck)


def _mem_attn_kernel(q_ref, kv_ref, o_ref):
    q = q_ref[0] * SCALE
    rows = q.shape[0]
    if rows < 8:
        q = jnp.broadcast_to(q[0:1], (8, MEM_WIDTH))
    for h in range(MEM_HEADS):
        sl = slice(h * HEAD_DIM, (h + 1) * HEAD_DIM)
        s = _mmd_nt(q[:, sl], kv_ref[0, :, sl])
        e = jnp.exp(s - jnp.max(s, axis=-1, keepdims=True))
        p = e / jnp.sum(e, axis=-1, keepdims=True)
        o = _mmd(p, kv_ref[0, :, MEM_WIDTH + h * HEAD_DIM:MEM_WIDTH + (h + 1) * HEAD_DIM])
        o_ref[0, :, sl] = o[:rows]


def _mem_attn(q, kv, tq):
    b, rows, _ = q.shape
    return pl.pallas_call(
        _mem_attn_kernel, grid=(b, rows // tq),
        in_specs=[pl.BlockSpec((1, tq, MEM_WIDTH), lambda b, i: (b, i, 0)),
                  pl.BlockSpec((1, N_MEM, 2 * MEM_WIDTH), lambda b, i: (b, 0, 0))],
        out_specs=pl.BlockSpec((1, tq, MEM_WIDTH), lambda b, i: (b, i, 0)),
        out_shape=jax.ShapeDtypeStruct((b, rows, MEM_WIDTH), F32), compiler_params=_cparams(2),
        name="mem_attn")(q, kv)


def _first_max(x, axis, n):
    idx = lax.broadcasted_iota(jnp.int32, x.shape, axis)
    m = jnp.max(x, axis=axis, keepdims=True)
    first = jnp.min(jnp.where(x == m, idx, n), axis=axis, keepdims=True)
    return m, idx == first, first


def _router_kernel(x_ref, w_ref, b_ref, e_ref, g_ref):
    tm = x_ref.shape[0]
    per = N_EXPERTS // N_EXPERT_GROUPS
    aff = jax.nn.sigmoid(_mmd_nt_precise(w_ref[...], x_ref[...]))
    biased = aff + b_ref[...]
    b3 = biased.reshape(N_EXPERT_GROUPS, per, tm)
    m1, pick1, _ = _first_max(b3, 1, per)
    m2 = jnp.max(jnp.where(pick1, -jnp.inf, b3), axis=1, keepdims=True)
    gs = (m1 + m2).reshape(N_EXPERT_GROUPS, tm)
    grp_on = jnp.zeros((N_EXPERT_GROUPS, tm), F32)
    for _ in range(TOPK_GROUPS):
        _, pick, _ = _first_max(gs, 0, N_EXPERT_GROUPS)
        grp_on = jnp.where(pick, 1.0, grp_on)
        gs = jnp.where(pick, -jnp.inf, gs)
    on = jnp.broadcast_to(grp_on.reshape(N_EXPERT_GROUPS, 1, tm), (N_EXPERT_GROUPS, per, tm)).reshape(N_EXPERTS, tm)
    cand = jnp.where(on > 0.5, biased, MASKED)
    gates, idxs = [], []
    for _ in range(TOP_K):
        _, pick, first = _first_max(cand, 0, N_EXPERTS)
        gates.append(jnp.sum(jnp.where(pick, aff, 0.0), axis=0, keepdims=True))
        idxs.append(first)
        cand = jnp.where(pick, -jnp.inf, cand)
    gate = jnp.concatenate(gates, axis=0)
    g_ref[...] = gate / jnp.sum(gate, axis=0, keepdims=True) * ROUTE_SCALE
    e_ref[...] = jnp.concatenate(idxs, axis=0)


def _router(x, w_t, b, tm):
    n = x.shape[0]
    return pl.pallas_call(
        _router_kernel, grid=(n // tm,),
        in_specs=[pl.BlockSpec((tm, D_MODEL), lambda i: (i, 0)), pl.BlockSpec((N_EXPERTS, D_MODEL), lambda i: (0, 0)),
                  pl.BlockSpec((N_EXPERTS, 1), lambda i: (0, 0))],
        out_specs=[pl.BlockSpec((TOP_K, tm), lambda i: (0, i)), pl.BlockSpec((TOP_K, tm), lambda i: (0, i))],
        out_shape=[jax.ShapeDtypeStruct((TOP_K, n), jnp.int32), jax.ShapeDtypeStruct((TOP_K, n), F32)],
        compiler_params=_cparams(1), name="router")(x, w_t, b.reshape(N_EXPERTS, 1))


def _ffn_kernel(be_ref, nv_ref, x_ref, wg_ref, wu_ref, wd_ref, rg_ref, o_ref):
    i = pl.program_id(0)

    @pl.when(i < nv_ref[0])
    def _():
        x = x_ref[...]
        h = jax.nn.silu(_mmd(x, wg_ref[0])) * _mmd(x, wu_ref[0])
        o_ref[...] = _mmd(h, wd_ref[0]) * rg_ref[...]

    @pl.when(i >= nv_ref[0])
    def _():
        o_ref[...] = jnp.zeros(o_ref.shape, F32)


def _ffn(xs, w_gate, w_up, w_down, blk_expert, n_valid, row_gate, blk):
    n_rows, d = xs.shape
    f = w_gate.shape[2]
    live = lambda i, nv: jnp.minimum(i, nv[0] - 1)
    grid_spec = pltpu.PrefetchScalarGridSpec(
        num_scalar_prefetch=2, grid=(n_rows // blk,),
        in_specs=[pl.BlockSpec((blk, d), lambda i, be, nv: (live(i, nv), 0)),
                  pl.BlockSpec((1, d, f), lambda i, be, nv: (be[live(i, nv)], 0, 0)),
                  pl.BlockSpec((1, d, f), lambda i, be, nv: (be[live(i, nv)], 0, 0)),
                  pl.BlockSpec((1, f, d), lambda i, be, nv: (be[live(i, nv)], 0, 0)),
                  pl.BlockSpec((blk, 1), lambda i, be, nv: (live(i, nv), 0))],
        out_specs=pl.BlockSpec((blk, d), lambda i, be, nv: (i, 0)))
    return pl.pallas_call(
        _ffn_kernel, grid_spec=grid_spec, out_shape=jax.ShapeDtypeStruct((n_rows, d), F32),
        compiler_params=_cparams(1), name="ffn")(blk_expert, n_valid, xs, w_gate, w_up, w_down, row_gate)


def _moe_out_kernel(x_ref, y_ref, sh_ref, g_ref, b_ref, o_ref):
    d = x_ref.shape[1]
    routed = y_ref[:, 0:d]
    for k in range(1, TOP_K):
        routed = routed + y_ref[:, k * d:(k + 1) * d]
    o_ref[...] = _layer_norm(DEEPNORM_ALPHA * x_ref[...] + (routed + sh_ref[...]), g_ref[...], b_ref[...])


def _moe_out(x, y8, shared, g, b, tm):
    n, d = x.shape
    return pl.pallas_call(
        _moe_out_kernel, grid=(n // tm,),
        in_specs=[pl.BlockSpec((tm, d), lambda i: (i, 0)), pl.BlockSpec((tm, TOP_K * d), lambda i: (i, 0)),
                  pl.BlockSpec((tm, d), lambda i: (i, 0)), pl.BlockSpec((1, d), lambda i: (0, 0)),
                  pl.BlockSpec((1, d), lambda i: (0, 0))],
        out_specs=pl.BlockSpec((tm, d), lambda i: (i, 0)), out_shape=jax.ShapeDtypeStruct((n, d), F32),
        compiler_params=_cparams(1), name="moe_out")(x, y8, shared, g.reshape(1, d), b.reshape(1, d))


def _moe_plan(eidx, gate, n, blk):
    nk = n * TOP_K
    flat_e = eidx.reshape(-1)
    order = jnp.argsort(flat_e, stable=True).astype(jnp.int32)
    sorted_e = flat_e[order]
    counts = jnp.sum((flat_e[:, None] == jnp.arange(N_EXPERTS)[None, :]).astype(jnp.int32), axis=0)
    padded = (counts + blk - 1) // blk * blk
    pad_end = jnp.cumsum(padded)
    pad_start = pad_end - padded
    start = jnp.cumsum(counts) - counts
    n_blocks = -(-(nk + N_EXPERTS * (blk - 1)) // blk)
    blk_expert = jnp.minimum(jnp.searchsorted(pad_end, jnp.arange(n_blocks) * blk, side='right'),
                             N_EXPERTS - 1).astype(jnp.int32)
    r = jnp.arange(n_blocks * blk)
    e_r = blk_expert[r // blk]
    off = r - pad_start[e_r]
    valid = (off >= 0) & (off < counts[e_r])
    src = order[jnp.clip(start[e_r] + off, 0, nk - 1)]
    row_tok = jnp.where(valid, src // TOP_K, n).astype(jnp.int32)
    row_gate = jnp.where(valid, gate.reshape(-1)[src], 0.0)
    dest_sorted = (pad_start[sorted_e] + jnp.arange(nk) - start[sorted_e]).astype(jnp.int32)
    dest = jnp.zeros((nk,), jnp.int32).at[order].set(dest_sorted)
    n_valid = (pad_end[-1] // blk).astype(jnp.int32).reshape(1)
    return row_tok, row_gate.reshape(-1, 1), blk_expert, n_valid, dest


def _bucket_lut():
    n = np.arange(LANES)
    nf = np.maximum(n, 1).astype(np.float32)
    exact = N_BUCKETS // 2
    large = exact + (np.log(nf / exact) / np.float32(math.log(MAX_DISTANCE / exact)) * (N_BUCKETS - exact)).astype(np.int32)
    return np.where(n < exact, n, np.minimum(large, N_BUCKETS - 1))


def _t5_table(rel_bias):
    return rel_bias[_bucket_lut()].T


def _even_params(w_in, cmp_w, cmp_pos):
    d = w_in.shape[0]
    n_g = 3 * NSA_HEADS
    g0 = (E_WV + NSA_KV) * LANES
    w = jnp.concatenate([w_in[:, :g0], w_in[:, g0 + n_g:], w_in[:, g0:g0 + n_g],
                         jnp.zeros((d, LANES - n_g), w_in.dtype)], axis=1).astype(_MM)
    half = CMP_STRIDE * HEAD_DIM
    top = cmp_w[:, :, :CMP_STRIDE].reshape(2, NSA_KV, half, HEAD_DIM)
    bot = cmp_w[:, :, CMP_STRIDE:].reshape(2, NSA_KV, half, HEAD_DIM)
    wcat = jnp.concatenate([top, bot], axis=-1).reshape(2 * NSA_KV, half, 2 * HEAD_DIM).astype(_MM)
    pp = cmp_pos.transpose(0, 2, 1, 3)
    ptop = pp[:, :, :CMP_STRIDE].reshape(2 * NSA_KV, 1, half)
    pbot = pp[:, :, CMP_STRIDE:].reshape(2 * NSA_KV, 1, half)
    pos2 = jnp.concatenate([ptop, pbot, jnp.zeros((2 * NSA_KV, 6, half), F32)], axis=1)
    return w, wcat, pos2


def _even_attend_prompt(p0, wcat, pos2, ftab, lam4, subln, lam_init, t, tq):
    h4 = p0[:, E_CK * LANES:E_SK * LANES].reshape(t, 2 * NSA_KV, HEAD_DIM).transpose(1, 0, 2)
    h4 = h4.reshape(2 * NSA_KV, t // CMP_STRIDE, CMP_STRIDE * HEAD_DIM)
    kvc = _compress_prompt(h4, wcat, pos2)
    ftab3 = ftab.reshape(NSA_HEADS + DIFF_HEADS, 1, LANES)
    o_cmp, selmask = _nsa_cmp_prompt(p0, kvc, ftab[:NSA_HEADS].reshape(NSA_KV, NSA_GROUP, LANES), t, min(tq, 256))
    o_sel = _nsa_sel_prompt(p0, selmask, ftab3, t, tq)
    o_nsa = _nsa_win_prompt(p0, o_cmp, o_sel, ftab3, t, tq)
    o_diff = _diff_prompt(p0, ftab3, lam4, subln.reshape(1, HEAD_DIM), t, tq, lam_init)
    return o_nsa, o_diff


def _rows_of_group(a0, a1):
    row = lax.broadcasted_iota(jnp.int32, a0.shape, 0)
    return jnp.where(row < NSA_GROUP, a0, a1)


def _softmax_with_new(s, s_new):
    m = jnp.maximum(jnp.max(s, axis=-1, keepdims=True), s_new)
    p = jnp.exp(s - m)
    p_new = jnp.exp(s_new - m)
    inv = 1.0 / (jnp.sum(p, axis=-1, keepdims=True) + p_new)
    return p * inv, p_new * inv


def _nsa_sample_kernel(pt_ref, *refs, pp, n_pages):
    page_refs = refs[:pp]
    (q_ref, new_ref, wnew_ref, wbuf_ref, g_ref, wcat_ref, pos2_ref, bcmp_ref, bsel_ref, bwin_ref, f0_ref,
     o_ref, hc_ref, ks_ref, vs_ref) = refs[pp:]
    j = pl.program_id(1)
    halves = PAGE_SIZE // CMP_STRIDE
    n_slot = 4 * NSA_KV
    for k in range(pp):
        pg = j * pp + k
        for l in range(CMP_STRIDE):
            for s in range(n_slot):
                piece = page_refs[k][0, :, (l * n_slot + s) * HEAD_DIM:(l * n_slot + s + 1) * HEAD_DIM]
                if s < 2 * NSA_KV:
                    hc_ref[s, pl.ds(pl.multiple_of(pg * halves, 8), halves), l * HEAD_DIM:(l + 1) * HEAD_DIM] = piece
                else:
                    dst = ks_ref if s < 3 * NSA_KV else vs_ref
                    dst[s % NSA_KV, pl.ds(pl.multiple_of(pg * PAGE_SIZE + l * halves, 8), halves), :] = piece

    @pl.when(j == pl.num_programs(1) - 1)
    def _():
        q = q_ref[0] * SCALE
        f0 = f0_ref[...]
        comp = [_compress(hc_ref[s], wcat_ref[s], pos2_ref[s]) for s in range(2 * NSA_KV)]
        lg = _rows_of_group(_mmd_nt(q, comp[0]), _mmd_nt(q, comp[1])) + bcmp_ref[...]
        e = jnp.exp(lg - jnp.max(lg, axis=-1, keepdims=True))
        p = e / jnp.sum(e, axis=-1, keepdims=True)
        o_cmp = _rows_of_group(_mmd(p, comp[2]), _mmd(p, comp[3]))
        n_blk = 2 * n_pages
        sc = _mmd_exact_rhs01(p, _sel_score_matrix(p.shape[1], n_blk))
        sc0 = jnp.sum(sc[:NSA_GROUP], axis=0, keepdims=True)
        sc1 = jnp.sum(sc[NSA_GROUP:], axis=0, keepdims=True)
        score = _rows_of_group(jnp.broadcast_to(sc0, sc.shape), jnp.broadcast_to(sc1, sc.shape))
        lane = lax.broadcasted_iota(jnp.int32, score.shape, 1)
        selm = _top_blocks(jnp.where(lane == 0, 1e9, score), N_SEL - 1).astype(_MM)
        ch = 4 * PAGE_SIZE
        n_ch = n_pages * PAGE_SIZE // ch
        eb = lax.broadcasted_iota(jnp.int32, (n_blk, ch), 0)
        ek = lax.broadcasted_iota(jnp.int32, (n_blk, ch), 1)
        blk_of_key = 2 * (ek // PAGE_SIZE) + (ek % halves) // (halves // 2)
        s_parts = []
        for c in range(n_ch):
            sl = slice(c * ch, (c + 1) * ch)
            s = _rows_of_group(_mmd_nt(q, ks_ref[0, sl, :]), _mmd_nt(q, ks_ref[1, sl, :]))
            expand = (eb == blk_of_key + 2 * (ch // PAGE_SIZE) * c).astype(_MM)
            picked = jnp.dot(selm, expand, preferred_element_type=F32) > 0.5
            s_parts.append(jnp.where(picked, s + bsel_ref[:, sl], -jnp.inf))
        s_all = jnp.concatenate(s_parts, axis=1)
        new = new_ref[0]
        hd = HEAD_DIM
        k_new = _rows_of_group(jnp.broadcast_to(new[:, 4 * hd:5 * hd], q.shape), jnp.broadcast_to(new[:, 5 * hd:6 * hd], q.shape))
        v_new = _rows_of_group(jnp.broadcast_to(new[:, 6 * hd:7 * hd], q.shape), jnp.broadcast_to(new[:, 7 * hd:8 * hd], q.shape))
        p, p_new = _softmax_with_new(s_all, jnp.sum(q * k_new, axis=-1, keepdims=True) + f0)
        o_sel = p_new * v_new
        for c in range(n_ch):
            sl = slice(c * ch, (c + 1) * ch)
            o_sel = o_sel + _rows_of_group(_mmd(p[:, sl], vs_ref[0, sl, :]), _mmd(p[:, sl], vs_ref[1, sl, :]))
        s = _rows_of_group(_mmd_nt(q, wbuf_ref[0, :, 0:hd]), _mmd_nt(q, wbuf_ref[0, :, hd:2 * hd])) + bwin_ref[...]
        wnew = wnew_ref[0]
        k_new = _rows_of_group(jnp.broadcast_to(wnew[:, 0:hd], q.shape), jnp.broadcast_to(wnew[:, hd:2 * hd], q.shape))
        v_new = _rows_of_group(jnp.broadcast_to(wnew[:, 2 * hd:3 * hd], q.shape), jnp.broadcast_to(wnew[:, 3 * hd:4 * hd], q.shape))
        p, p_new = _softmax_with_new(s, jnp.sum(q * k_new, axis=-1, keepdims=True) + f0)
        o_win = p_new * v_new + _rows_of_group(_mmd(p, wbuf_ref[0, :, 2 * hd:3 * hd]), _mmd(p, wbuf_ref[0, :, 3 * hd:4 * hd]))
        gates = jnp.broadcast_to(jax.nn.sigmoid(g_ref[0]), (NSA_HEADS, LANES))
        lane = lax.broadcasted_iota(jnp.int32, (NSA_HEADS, LANES), 1)
        row = lax.broadcasted_iota(jnp.int32, (NSA_HEADS, LANES), 0)
        gate = lambda k: jnp.sum(jnp.where(lane == 3 * row + k, gates, 0.0), axis=-1, keepdims=True)
        o_ref[0] = gate(0) * o_cmp + gate(1) * o_sel + gate(2) * o_win


def _nsa_sample(pool, page_table, q, new_rows, win_new, win_buf, g_raw, wcat, pos2, ftab, pp):
    b, n_pages = page_table.shape
    t_past = n_pages * PAGE_SIZE
    halves = PAGE_SIZE // CMP_STRIDE
    n_c = n_pages * halves
    assert win_buf.shape[1] == WINDOW and n_pages % pp == 0 and (t_past // (4 * PAGE_SIZE)) * 4 * PAGE_SIZE == t_past
    width = 4 * NSA_KV * HEAD_DIM
    pool_v = pool.reshape(pool.shape[0], halves, CMP_STRIDE * width)
    f = ftab[:NSA_HEADS]
    ninf = jnp.float32(-jnp.inf)
    c = np.arange(n_c)
    rel_c = t_past - (c * CMP_STRIDE + CMP_LEN - 1)
    bcmp = jnp.where((rel_c >= 0) & (c < n_c - 1), f[:, np.clip(rel_c, 0, LANES - 1)], ninf)
    kk = np.arange(t_past)
    pos = (kk // PAGE_SIZE) * PAGE_SIZE + CMP_STRIDE * (kk % halves) + (kk % PAGE_SIZE) // halves
    bsel = f[:, np.clip(t_past - pos, 0, LANES - 1)]
    wi = np.arange(WINDOW)
    bwin = jnp.where(wi >= 1, f[:, np.clip(WINDOW - wi, 0, LANES - 1)], ninf)
    f0 = f[:, 0:1]
    page_spec = lambda k: pl.BlockSpec((1, halves, CMP_STRIDE * width), lambda s, j, pt: (pt[s, j * pp + k], 0, 0))
    full = lambda a: pl.BlockSpec(a.shape, lambda s, j, pt: (0,) * a.ndim)
    per_seq = lambda a: pl.BlockSpec((1,) + a.shape[1:], lambda s, j, pt: (s,) + (0,) * (a.ndim - 1))
    ops = [q, new_rows, win_new, win_buf, g_raw]
    consts = [wcat, pos2, bcmp, bsel, bwin, f0]
    grid_spec = pltpu.PrefetchScalarGridSpec(
        num_scalar_prefetch=1, grid=(b, n_pages // pp),
        in_specs=[page_spec(k) for k in range(pp)] + [per_seq(a) for a in ops] + [full(a) for a in consts],
        out_specs=pl.BlockSpec((1, NSA_HEADS, HEAD_DIM), lambda s, j, pt: (s, 0, 0)),
        scratch_shapes=[pltpu.VMEM((2 * NSA_KV, n_c, CMP_STRIDE * HEAD_DIM), F32),
                        pltpu.VMEM((NSA_KV, t_past, HEAD_DIM), F32), pltpu.VMEM((NSA_KV, t_past, HEAD_DIM), F32)])
    return pl.pallas_call(
        functools.partial(_nsa_sample_kernel, pp=pp, n_pages=n_pages), grid_spec=grid_spec,
        out_shape=jax.ShapeDtypeStruct((b, NSA_HEADS, HEAD_DIM), F32), compiler_params=_cparams(2),
        name="nsa_sample")(page_table, *([pool_v] * pp), *ops, *consts)


def _diag_rows(x, n_heads):
    lane_blk = lax.broadcasted_iota(jnp.int32, x.shape, 1) // HEAD_DIM
    row = lax.broadcasted_iota(jnp.int32, x.shape, 0)
    return jnp.where(lane_blk == row % n_heads, x, 0.0)


def _diff_sample_kernel(pt_ref, *refs, pp, n_pages, lam_init):
    page_refs = refs[:pp]
    q_ref, kn_ref, vn_ref, blast_ref, f_ref, lam_ref, sub_ref, o_ref, m_ref, l_ref, acc_ref = refs[pp:]
    j = pl.program_id(1)
    w = DIFF_HEADS * HEAD_DIM
    q = q_ref[0] * DIFF_SCALE

    @pl.when(j == 0)
    def _():
        m_ref[...] = jnp.sum(q * kn_ref[0], axis=-1, keepdims=True) + f_ref[:, 0:1]
        l_ref[...] = jnp.ones(l_ref.shape, F32)
        acc_ref[...] = jnp.broadcast_to(vn_ref[0], acc_ref.shape)

    for k in range(pp):
        pg = j * pp + k
        bias = jnp.where(pg == n_pages - 1, blast_ref[...], f_ref[:, LANES - 1:LANES])
        s = _mmd_nt(q, page_refs[k][0, :, :w]) + bias
        m, l, acc = _online_update((m_ref[...], l_ref[...], acc_ref[...]), s, page_refs[k][0, :, w:])
        m_ref[...] = m
        l_ref[...] = l
        acc_ref[...] = acc

    @pl.when(j == pl.num_programs(1) - 1)
    def _():
        o12 = acc_ref[...] / l_ref[...]
        lam = _diff_lambda(lam_ref[...], lam_init)
        o = _diag_rows(o12[:DIFF_HEADS] - lam * o12[DIFF_HEADS:], DIFF_HEADS)
        ms = jnp.sum(o * o, axis=-1, keepdims=True) / HEAD_DIM
        o = o * lax.rsqrt(ms + LN_EPS) * sub_ref[...] * (1.0 - lam_init)
        o_ref[0] = jnp.sum(o, axis=0, keepdims=True)


def _block_diag_rows(q):
    b, h, d = q.shape
    return (q[:, :, None, :] * jnp.eye(h, dtype=q.dtype)[None, :, :, None]).reshape(b, h, h * d)


def _diff_sample(pool, page_table, q, k_new, v_new, ftab, lam4, subln, lam_init, pp):
    b, n_pages = page_table.shape
    w = DIFF_HEADS * HEAD_DIM
    lo = jnp.arange(HEAD_DIM) < DIFF_QK
    qbd = jnp.concatenate([_block_diag_rows(jnp.where(lo, q, 0.0)), _block_diag_rows(jnp.where(lo, 0.0, q))], axis=1)
    f = ftab[NSA_HEADS:]
    f2 = jnp.concatenate([f, f], axis=0)
    blast = f2[:, np.clip(PAGE_SIZE - np.arange(PAGE_SIZE), 0, LANES - 1)]
    sub_t = jnp.tile(subln.reshape(1, HEAD_DIM), (1, DIFF_HEADS))
    page_spec = lambda k: pl.BlockSpec((1, PAGE_SIZE, 2 * w), lambda s, j, pt: (pt[s, j * pp + k], 0, 0))
    full = lambda a: pl.BlockSpec(a.shape, lambda s, j, pt: (0,) * a.ndim)
    per_seq = lambda a: pl.BlockSpec((1,) + a.shape[1:], lambda s, j, pt: (s,) + (0,) * (a.ndim - 1))
    ops = [qbd, k_new, v_new]
    consts = [blast, f2, lam4, sub_t]
    grid_spec = pltpu.PrefetchScalarGridSpec(
        num_scalar_prefetch=1, grid=(b, n_pages // pp),
        in_specs=[page_spec(k) for k in range(pp)] + [per_seq(a) for a in ops] + [full(a) for a in consts],
        out_specs=pl.BlockSpec((1, 1, w), lambda s, j, pt: (s, 0, 0)),
        scratch_shapes=[pltpu.VMEM((2 * DIFF_HEADS, 1), F32), pltpu.VMEM((2 * DIFF_HEADS, 1), F32),
                        pltpu.VMEM((2 * DIFF_HEADS, w), F32)])
    return pl.pallas_call(
        functools.partial(_diff_sample_kernel, pp=pp, n_pages=n_pages, lam_init=lam_init), grid_spec=grid_spec,
        out_shape=jax.ShapeDtypeStruct((b, 1, w), F32), compiler_params=_cparams(2),
        name="diff_sample")(page_table, *([pool] * pp), *ops, *consts)


def _fox_sample_kernel(pt_ref, *refs, pp):
    kv_refs, lf_refs = refs[:pp], refs[pp:2 * pp]
    q_ref, kn_ref, vn_ref, lfn_ref, o_ref, m_ref, l_ref, acc_ref, suf_ref = refs[2 * pp:]
    j = pl.program_id(1)
    w = FOX_HEADS * HEAD_DIM
    q = q_ref[0] * SCALE

    @pl.when(j == 0)
    def _():
        m_ref[...] = jnp.sum(q * kn_ref[0], axis=-1, keepdims=True)
        l_ref[...] = jnp.ones(l_ref.shape, F32)
        acc_ref[...] = jnp.broadcast_to(vn_ref[0], acc_ref.shape)
        suf_ref[...] = lfn_ref[0]

    r = lax.broadcasted_iota(jnp.int32, (PAGE_SIZE, PAGE_SIZE), 0)
    c = lax.broadcasted_iota(jnp.int32, (PAGE_SIZE, PAGE_SIZE), 1)
    later = (r > c).astype(_MM)
    for k in range(pp):
        lf = lf_refs[k][0]
        bias = _mmd_exact_rhs01(lf, later) + suf_ref[...]
        s = _mmd_nt(q, kv_refs[k][0, :, :w]) + bias
        m, l, acc = _online_update((m_ref[...], l_ref[...], acc_ref[...]), s, kv_refs[k][0, :, w:])
        m_ref[...] = m
        l_ref[...] = l
        acc_ref[...] = acc
        suf_ref[...] = suf_ref[...] + jnp.sum(lf, axis=-1, keepdims=True)

    @pl.when(j == pl.num_programs(1) - 1)
    def _():
        o_ref[0] = jnp.sum(_diag_rows(acc_ref[...] / l_ref[...], FOX_HEADS), axis=0, keepdims=True)


def _fox_sample(kv_pool, lf_pool_t, page_table, q, k_new, v_new, lf_new, pp):
    b, n_pages = page_table.shape
    w = FOX_HEADS * HEAD_DIM
    qbd = _block_diag_rows(q)
    back = lambda j, k: n_pages - 1 - (j * pp + k)
    kv_spec = lambda k: pl.BlockSpec((1, PAGE_SIZE, 2 * w), lambda s, j, pt: (pt[s, back(j, k)], 0, 0))
    lf_spec = lambda k: pl.BlockSpec((1, FOX_HEADS, PAGE_SIZE), lambda s, j, pt: (pt[s, back(j, k)], 0, 0))
    per_seq = lambda a: pl.BlockSpec((1,) + a.shape[1:], lambda s, j, pt: (s,) + (0,) * (a.ndim - 1))
    ops = [qbd, k_new, v_new, lf_new]
    grid_spec = pltpu.PrefetchScalarGridSpec(
        num_scalar_prefetch=1, grid=(b, n_pages // pp),
        in_specs=[kv_spec(k) for k in range(pp)] + [lf_spec(k) for k in range(pp)] + [per_seq(a) for a in ops],
        out_specs=pl.BlockSpec((1, 1, w), lambda s, j, pt: (s, 0, 0)),
        scratch_shapes=[pltpu.VMEM((FOX_HEADS, 1), F32), pltpu.VMEM((FOX_HEADS, 1), F32),
                        pltpu.VMEM((FOX_HEADS, w), F32), pltpu.VMEM((FOX_HEADS, 1), F32)])
    return pl.pallas_call(
        functools.partial(_fox_sample_kernel, pp=pp), grid_spec=grid_spec,
        out_shape=jax.ShapeDtypeStruct((b, 1, w), F32), compiler_params=_cparams(2),
        name="fox_sample")(page_table, *([kv_pool] * pp), *([lf_pool_t] * pp), *ops)


TQ = 512
TM = 256
PAGES_PER_STEP = 4


def _memory_layer(hp, hs, mem_kv_p, mem_kv_s, w_q, w_o, g, b):
    bs = hs.shape[0]
    w_q = w_q.astype(_MM)
    w_o = w_o.astype(_MM)
    qp = _mm(hp, w_q, TQ, MEM_WIDTH)
    op = _mem_attn(qp[None], mem_kv_p[None], TQ)[0]
    hp = _mm_ln([op], [w_o], hp, g, b, TM)
    qs = _mm(hs, w_q, bs, MEM_WIDTH)
    os_ = _mem_attn(qs.reshape(bs, 1, MEM_WIDTH), mem_kv_s, 1).reshape(bs, MEM_WIDTH)
    hs = _mm_ln([os_], [w_o], hs, g, b, bs)
    return hp, hs


def _moe_layer(hp, hs, w_router, b_router, w_gate, w_up, w_down, ws_gate, ws_up, ws_down, g, b):
    t, d = hp.shape
    bs = hs.shape[0]
    n = t + bs
    w_rt = w_router.T
    ep, gp = _router(hp, w_rt, b_router, TQ)
    es, gs = _router(hs, w_rt, b_router, bs)
    eidx = jnp.concatenate([ep, es], axis=1).T
    gate = jnp.concatenate([gp, gs], axis=1).T
    row_tok, row_gate, blk_expert, n_valid, dest = _moe_plan(eidx, gate, n, MOE_BLK)
    x_all = jnp.concatenate([hp, hs, jnp.zeros((1, d), F32)], axis=0).astype(_MM)
    xs = x_all[row_tok]
    ys = _ffn(xs, w_gate, w_up, w_down, blk_expert, n_valid, row_gate, MOE_BLK)
    y8 = ys[dest].reshape(n, TOP_K * d)
    one = lambda m, blk: (jnp.zeros((m // blk,), jnp.int32), jnp.full((1,), m // blk, jnp.int32), jnp.ones((m, 1), F32))
    sh_p = _ffn(hp, ws_gate[None], ws_up[None], ws_down[None], *one(t, TQ), TQ)
    sh_s = _ffn(hs, ws_gate[None], ws_up[None], ws_down[None], *one(bs, bs), bs)
    hp = _moe_out(hp, y8[:t], sh_p, g, b, LANES)
    hs = _moe_out(hs, y8[t:], sh_s, g, b, bs)
    return hp, hs


def kernel(x_prompt, x_sample, cache_nsa_kv, cache_nsa_win, cache_diff_kv, cache_fox_kv, cache_fox_logf, cache_mem_kv, page_table, mem_prompt, rel_bias, w_even_in, w_even_out, nsa_cmp_pos, nsa_cmp_w, diff_lambda, diff_subln, w_odd_in, b_forget, w_odd_out, w_mem_q, w_mem_kv, w_mem_o, ln_g, ln_b, w_router, b_router, w_exp_gate, w_exp_up, w_exp_down, w_sh_gate, w_sh_up, w_sh_down):
    bp, t, d = x_prompt.shape
    bs = x_sample.shape[0]
    assert bp == 1 and x_sample.shape[1] == 1 and t % TQ == 0
    n_pool = cache_nsa_kv.shape[1]
    wb = cache_nsa_win.shape[2]
    hp = x_prompt.reshape(t, d)
    hs = x_sample.reshape(bs, d)
    ftab = _t5_table(rel_bias)
    L = LANES
    p_nsa, p_win, p_diff, p_fox, p_logf, p_mem = [], [], [], [], [], []
    s_nsa, s_win, s_diff, s_fox, s_logf = [], [], [], [], []
    for layer in range(DEPTH):
        i = layer // 2
        if layer % 2 == 0:
            lam_init = 0.8 - 0.6 * math.exp(-0.3 * layer)
            w_in, wcat, pos2 = _even_params(w_even_in[i], nsa_cmp_w[i], nsa_cmp_pos[i])
            w_out = w_even_out[i].astype(_MM)
            nsa_w = NSA_HEADS * HEAD_DIM
            p0 = _mm(hp, w_in, TQ, E_COLS // 5)
            p0s = _mm(hs, w_in, bs, E_COLS // 5)
            o_nsa, o_diff = _even_attend_prompt(p0, wcat, pos2, ftab, diff_lambda[i], diff_subln[i], lam_init, t, TQ)
            nsa_new = p0s[:, E_CK * L:E_WK * L]
            win_new = p0s[:, E_WK * L:E_DQ * L]
            dk_new = p0s[:, E_DK * L:E_DV * L]
            dv_new = p0s[:, E_DV * L:E_G * L]
            o_nsa_s = _nsa_sample(
                cache_nsa_kv[i].reshape(n_pool, PAGE_SIZE, 4 * NSA_KV * HEAD_DIM), page_table,
                p0s[:, :E_CK * L].reshape(bs, NSA_HEADS, HEAD_DIM), nsa_new[:, None], win_new[:, None],
                cache_nsa_win[i].reshape(bs, wb, 2 * NSA_KV * HEAD_DIM), p0s[:, E_G * L:].reshape(bs, 1, L),
                wcat, pos2, ftab, PAGES_PER_STEP).reshape(bs, nsa_w)
            o_diff_s = _diff_sample(
                cache_diff_kv[i].reshape(n_pool, PAGE_SIZE, 2 * DIFF_HEADS * HEAD_DIM), page_table,
                p0s[:, E_DQ * L:E_DK * L].reshape(bs, DIFF_HEADS, HEAD_DIM), dk_new[:, None], dv_new[:, None],
                ftab, diff_lambda[i], diff_subln[i], lam_init, PAGES_PER_STEP).reshape(bs, DIFF_HEADS * HEAD_DIM)
            w_outs = [w_out[:nsa_w], w_out[nsa_w:]]
            hp = _mm_ln([o_nsa, o_diff], w_outs, hp, ln_g[layer, 0], ln_b[layer, 0], TM)
            hs = _mm_ln([o_nsa_s, o_diff_s], w_outs, hs, ln_g[layer, 0], ln_b[layer, 0], bs)
            p_nsa.append(p0[:, E_CK * L:E_WK * L].reshape(1, t, 4, NSA_KV, HEAD_DIM))
            p_win.append(p0[t - min(WINDOW, t):, E_WK * L:E_DQ * L].reshape(1, min(WINDOW, t), 2, NSA_KV, HEAD_DIM))
            p_diff.append(p0[:, E_DK * L:E_G * L].reshape(1, t, 2, DIFF_HEADS, HEAD_DIM))
            s_nsa.append(nsa_new.reshape(bs, 1, 4, NSA_KV, HEAD_DIM))
            s_win.append(jnp.concatenate([cache_nsa_win[i][:, 1:], win_new.reshape(bs, 1, 2, NSA_KV, HEAD_DIM)], axis=1))
            s_diff.append(jnp.concatenate([dk_new, dv_new], axis=1).reshape(bs, 1, 2, DIFF_HEADS, HEAD_DIM))
        else:
            fox_w = FOX_HEADS * HEAD_DIM
            w_in = jnp.pad(w_odd_in[i], ((0, 0), (0, O_COLS - w_odd_in.shape[2]))).astype(_MM)
            p1 = _mm(hp, w_in, TQ, O_COLS // 7)
            p1s = _mm(hs, w_in, bs, O_COLS // 7)
            logf, cum = _fox_prep(p1, b_forget[i], TM)
            logf_s, _ = _fox_prep(p1s, b_forget[i], bs)
            o = _fox_prompt(p1, cum, t, TQ)
            o_s = _fox_sample(
                cache_fox_kv[i].reshape(n_pool, PAGE_SIZE, 2 * fox_w), cache_fox_logf[i].astype(F32).transpose(0, 2, 1),
                page_table, p1s[:, :fox_w].reshape(bs, FOX_HEADS, HEAD_DIM), p1s[:, None, fox_w:2 * fox_w],
                p1s[:, None, 2 * fox_w:3 * fox_w], logf_s[:, :FOX_HEADS, None], PAGES_PER_STEP).reshape(bs, fox_w)
            w_out = w_odd_out[i].astype(_MM)
            hp = _mm_ln([o], [w_out], hp, ln_g[layer, 0], ln_b[layer, 0], TM)
            hs = _mm_ln([o_s], [w_out], hs, ln_g[layer, 0], ln_b[layer, 0], bs)
            p_fox.append(p1[:, fox_w:3 * fox_w].reshape(1, t, 2, FOX_HEADS, HEAD_DIM))
            p_logf.append(logf[:, :FOX_HEADS].reshape(1, t, FOX_HEADS))
            s_fox.append(p1s[:, fox_w:3 * fox_w].reshape(bs, 1, 2, FOX_HEADS, HEAD_DIM))
            s_logf.append(logf_s[:, :FOX_HEADS].reshape(bs, 1, FOX_HEADS))
        mem_kv = _mm(mem_prompt.reshape(N_MEM, d), w_mem_kv[layer].astype(_MM), N_MEM, 2 * MEM_WIDTH)
        p_mem.append(mem_kv.reshape(1, N_MEM, 2, MEM_HEADS, HEAD_DIM))
        hp, hs = _memory_layer(hp, hs, mem_kv, cache_mem_kv[layer].reshape(bs, N_MEM, 2 * MEM_WIDTH),
                               w_mem_q[layer], w_mem_o[layer], ln_g[layer, 1], ln_b[layer, 1])
        hp, hs = _moe_layer(hp, hs, w_router[layer], b_router[layer], w_exp_gate[layer], w_exp_up[layer],
                            w_exp_down[layer], w_sh_gate[layer], w_sh_up[layer], w_sh_down[layer],
                            ln_g[layer, 2], ln_b[layer, 2])
    return (hp.reshape(1, t, d), hs.reshape(bs, 1, d), jnp.stack(p_nsa), jnp.stack(p_win), jnp.stack(p_diff),
            jnp.stack(p_fox), jnp.stack(p_logf), jnp.stack(p_mem), jnp.stack(s_nsa), jnp.stack(s_win),
            jnp.stack(s_diff), jnp.stack(s_fox), jnp.stack(s_logf))
```

```python
import functools
import math

import numpy as np
import jax
import jax.numpy as jnp
from jax import lax
from jax.experimental import pallas as pl
from jax.experimental.pallas import tpu as pltpu

F32 = jnp.float32
_MM = jnp.bfloat16

D_MODEL = 2048
HEAD_DIM = 128
NSA_HEADS = 8
NSA_KV = 2
NSA_GROUP = NSA_HEADS // NSA_KV
CMP_STRIDE = 16
CMP_LEN = 2 * CMP_STRIDE
SEL_BLOCK = 64
CMP_PER_SEL = SEL_BLOCK // CMP_STRIDE
N_SEL = 16
WINDOW = 512
DIFF_HEADS = 8
DIFF_QK = HEAD_DIM // 2
FOX_HEADS = 16
N_MEM = 256
MEM_HEADS = 4
MEM_WIDTH = MEM_HEADS * HEAD_DIM
N_BUCKETS = 32
MAX_DISTANCE = 128
N_EXPERTS = 64
N_EXPERT_GROUPS = 8
TOPK_GROUPS = 4
TOP_K = 8
D_EXPERT = 512
ROUTE_SCALE = 2.5
PAGE_SIZE = 128
DEPTH = 2
DEEPNORM_ALPHA = (2 * DEPTH) ** 0.25
LN_EPS = 1e-5
MASKED = -1e30
SCALE = HEAD_DIM ** -0.5
DIFF_SCALE = DIFF_QK ** -0.5
LANES = 128
SUBLANES = 8
VMEM_LIMIT = 56 * 1024 * 1024
MOE_BLK = 256

E_NQ, E_CK, E_CV, E_SK, E_SV, E_WK, E_WV, E_DQ, E_DK, E_DV, E_G = 0, 8, 10, 12, 14, 16, 18, 20, 28, 36, 44
E_COLS = 45 * LANES
O_Q, O_K, O_V, O_F = 0, 16, 32, 48
O_COLS = 49 * LANES


def _cparams(n_grid):
    return pltpu.CompilerParams(dimension_semantics=("arbitrary",) * n_grid, vmem_limit_bytes=VMEM_LIMIT)


def _mmd(a, b):
    return jnp.dot(a.astype(_MM), b.astype(_MM), preferred_element_type=F32)


def _mmd_nt(a, b):
    return lax.dot_general(a.astype(_MM), b.astype(_MM), (((1,), (1,)), ((), ())), preferred_element_type=F32)


def _split3(x):
    hi = x.astype(_MM)
    r = x - hi.astype(F32)
    mid = r.astype(_MM)
    lo = (r - mid.astype(F32)).astype(_MM)
    return hi, mid, lo


def _mmd_nt_precise(a, b):
    nt = lambda x, y: lax.dot_general(x, y, (((1,), (1,)), ((), ())), preferred_element_type=F32)
    a0, a1, a2 = _split3(a)
    b0, b1, b2 = _split3(b)
    return ((nt(a2, b0) + nt(a0, b2) + nt(a1, b1)) + (nt(a1, b0) + nt(a0, b1))) + nt(a0, b0)


def _mmd_exact_rhs01(x, m01):
    hi, mid, lo = _split3(x)
    f = lambda a: jnp.dot(a, m01, preferred_element_type=F32)
    return f(hi) + f(mid) + f(lo)


def _mmd_exact_lhs01(m01, x):
    hi, mid, lo = _split3(x)
    f = lambda a: jnp.dot(m01, a, preferred_element_type=F32)
    return f(hi) + f(mid) + f(lo)


def _t5_bias(tbl, rel):
    idx = jnp.clip(rel, 0, LANES - 1)
    parts = [jnp.take_along_axis(tbl, idx[:, c * LANES:(c + 1) * LANES], axis=1) for c in range(rel.shape[1] // LANES)]
    return parts[0] if len(parts) == 1 else jnp.concatenate(parts, axis=1)


def _online_update(carry, s, v):
    m, l, acc = carry
    m_new = jnp.maximum(m, jnp.max(s, axis=-1, keepdims=True))
    alpha = jnp.exp(m - m_new)
    p = jnp.exp(s - m_new)
    l_new = alpha * l + jnp.sum(p, axis=-1, keepdims=True)
    acc_new = alpha * acc + _mmd(p, v)
    return m_new, l_new, acc_new


def _flash_init(rows, width):
    return (jnp.full((rows, 1), MASKED, F32), jnp.zeros((rows, 1), F32), jnp.zeros((rows, width), F32))


def _layer_norm(y, g, b):
    mu = jnp.mean(y, axis=-1, keepdims=True)
    yc = y - mu
    var = jnp.mean(yc * yc, axis=-1, keepdims=True)
    return yc * lax.rsqrt(var + LN_EPS) * g + b


def _mm_kernel(x_ref, w_ref, o_ref):
    o_ref[...] = _mmd(x_ref[...], w_ref[...])


def _mm(x, w, tm, tn):
    m, k = x.shape
    n = w.shape[1]
    return pl.pallas_call(
        _mm_kernel, grid=(n // tn, m // tm),
        in_specs=[pl.BlockSpec((tm, k), lambda j, i: (i, 0)), pl.BlockSpec((k, tn), lambda j, i: (0, j))],
        out_specs=pl.BlockSpec((tm, tn), lambda j, i: (i, j)),
        out_shape=jax.ShapeDtypeStruct((m, n), F32), compiler_params=_cparams(2), name="mm")(x, w)


def _mm_ln_kernel(*refs, n_in):
    xs, ws = refs[:n_in], refs[n_in:2 * n_in]
    res_ref, g_ref, b_ref, o_ref = refs[2 * n_in:]
    acc = _mmd(xs[0][...], ws[0][...])
    for x_ref, w_ref in zip(xs[1:], ws[1:]):
        acc = acc + _mmd(x_ref[...], w_ref[...])
    o_ref[...] = _layer_norm(DEEPNORM_ALPHA * res_ref[...] + acc, g_ref[...], b_ref[...])


def _mm_ln(xs, ws, res, g, b, tm):
    m, d = res.shape
    n_in = len(xs)
    in_specs = [pl.BlockSpec((tm, x.shape[1]), lambda i: (i, 0)) for x in xs]
    in_specs += [pl.BlockSpec(w.shape, lambda i: (0, 0)) for w in ws]
    in_specs += [pl.BlockSpec((tm, d), lambda i: (i, 0)), pl.BlockSpec((1, d), lambda i: (0, 0)),
                 pl.BlockSpec((1, d), lambda i: (0, 0))]
    return pl.pallas_call(
        functools.partial(_mm_ln_kernel, n_in=n_in), grid=(m // tm,), in_specs=in_specs,
        out_specs=pl.BlockSpec((tm, d), lambda i: (i, 0)), out_shape=jax.ShapeDtypeStruct((m, d), F32),
        compiler_params=_cparams(1), name="mm_ln")(*xs, *ws, res, g.reshape(1, d), b.reshape(1, d))


def _compress(h, wcat, pos2):
    n = h.shape[0]
    ab = _mmd(h, wcat)
    r = _mmd(pos2, wcat)
    const = r[0:1, :HEAD_DIM] + r[1:2, HEAD_DIM:]
    b_next = pltpu.roll(ab[:, HEAD_DIM:], n - 1, 0)
    return ab[:, :HEAD_DIM] + b_next + const


def _compress_kernel(h_ref, w_ref, p_ref, o_ref):
    o_ref[0] = _compress(h_ref[0], w_ref[0], p_ref[0])


def _compress_prompt(h4, wcat, pos2):
    n = h4.shape[1]
    return pl.pallas_call(
        _compress_kernel, grid=(4,),
        in_specs=[pl.BlockSpec((1, n, 16 * HEAD_DIM), lambda s: (s, 0, 0)),
                  pl.BlockSpec((1, 16 * HEAD_DIM, 2 * HEAD_DIM), lambda s: (s, 0, 0)),
                  pl.BlockSpec((1, 8, 16 * HEAD_DIM), lambda s: (s, 0, 0))],
        out_specs=pl.BlockSpec((1, n, HEAD_DIM), lambda s: (s, 0, 0)),
        out_shape=jax.ShapeDtypeStruct((4, n, HEAD_DIM), F32), compiler_params=_cparams(1),
        name="nsa_compress")(h4, wcat, pos2)


def _sel_score_matrix(n_cmp_pad, n_blk):
    c = lax.broadcasted_iota(jnp.int32, (n_cmp_pad, n_blk), 0)
    b = lax.broadcasted_iota(jnp.int32, (n_cmp_pad, n_blk), 1)
    return ((c >= CMP_PER_SEL * b - 1) & (c <= CMP_PER_SEL * b + CMP_PER_SEL - 1)).astype(_MM)


def _top_blocks(score, n_pick):
    blk = lax.broadcasted_iota(jnp.int32, score.shape, 1)
    sel = jnp.zeros(score.shape, F32)
    for _ in range(n_pick):
        m = jnp.max(score, axis=-1, keepdims=True)
        first = jnp.min(jnp.where(score == m, blk, score.shape[1]), axis=-1, keepdims=True)
        pick = blk == first
        sel = jnp.where(pick, 1.0, sel)
        score = jnp.where(pick, -jnp.inf, score)
    return sel


def _nsa_cmp_kernel(f_ref, q_ref, kc_ref, vc_ref, o_ref, sel_ref, *, tq, n_cmp):
    i = pl.program_id(1)
    n_pad = kc_ref.shape[1]
    n_blk = sel_ref.shape[2]
    kc = kc_ref[0].astype(_MM)
    vc = vc_ref[0].astype(_MM)
    qpos_c = i * tq + lax.broadcasted_iota(jnp.int32, (tq, n_pad), 0)
    c = lax.broadcasted_iota(jnp.int32, (tq, n_pad), 1)
    rel = qpos_c - (c * CMP_STRIDE + CMP_LEN - 1)
    vis = (rel >= 0) & (c < n_cmp)
    psum = jnp.zeros((tq, n_pad), F32)
    for n in range(NSA_GROUP):
        qh = q_ref[:, n * HEAD_DIM:(n + 1) * HEAD_DIM] * SCALE
        tbl = jnp.broadcast_to(f_ref[0, n:n + 1, :], (tq, LANES))
        lg = jnp.where(vis, _mmd_nt(qh, kc) + _t5_bias(tbl, rel), MASKED)
        m = jnp.max(lg, axis=-1, keepdims=True)
        e = jnp.where(vis, jnp.exp(lg - m), 0.0)
        p = e / jnp.maximum(jnp.sum(e, axis=-1, keepdims=True), 1e-30)
        o_ref[:, n * HEAD_DIM:(n + 1) * HEAD_DIM] = _mmd(p, vc)
        psum = psum + p
    score = _mmd_exact_rhs01(psum, _sel_score_matrix(n_pad, n_blk))
    blk = lax.broadcasted_iota(jnp.int32, (tq, n_blk), 1)
    qpos = i * tq + lax.broadcasted_iota(jnp.int32, (tq, n_blk), 0)
    forced = (blk == qpos // SEL_BLOCK) | (blk == 0)
    valid = blk * SEL_BLOCK <= qpos
    score = jnp.where(forced, 1e9, jnp.where(valid, score, -1.0))
    sel_ref[0] = _top_blocks(score, min(N_SEL, n_blk))


def _nsa_cmp_prompt(p0, kvc, ftab, t, tq):
    n_pad = kvc.shape[1]
    n_blk = t // SEL_BLOCK
    return pl.pallas_call(
        functools.partial(_nsa_cmp_kernel, tq=tq, n_cmp=t // CMP_STRIDE - 1), grid=(NSA_KV, t // tq),
        in_specs=[pl.BlockSpec((1, NSA_GROUP, LANES), lambda g, i: (g, 0, 0)),
                  pl.BlockSpec((tq, NSA_GROUP * HEAD_DIM), lambda g, i: (i, g)),
                  pl.BlockSpec((1, n_pad, HEAD_DIM), lambda g, i: (g, 0, 0)),
                  pl.BlockSpec((1, n_pad, HEAD_DIM), lambda g, i: (NSA_KV + g, 0, 0))],
        out_specs=[pl.BlockSpec((tq, NSA_GROUP * HEAD_DIM), lambda g, i: (i, g)),
                   pl.BlockSpec((1, tq, n_blk), lambda g, i: (g, i, 0))],
        out_shape=[jax.ShapeDtypeStruct((t, NSA_HEADS * HEAD_DIM), F32),
                   jax.ShapeDtypeStruct((NSA_KV, t, n_blk), F32)],
        compiler_params=_cparams(2), name="nsa_cmp")(ftab, p0, kvc, kvc)


def _nsa_sel_kernel(f_ref, q_ref, k_ref, v_ref, sel_ref, o_ref, *, tq):
    i = pl.program_id(1)
    q = q_ref[...] * SCALE
    selm = sel_ref[0].astype(_MM)
    n_blk = selm.shape[1]
    tbl = jnp.broadcast_to(f_ref[0], (tq, LANES))
    far_bias = f_ref[0][:, LANES - 1:LANES]
    blk_r = lax.broadcasted_iota(jnp.int32, (n_blk, tq), 0)
    blk_c = lax.broadcasted_iota(jnp.int32, (n_blk, tq), 1) // SEL_BLOCK
    row = lax.broadcasted_iota(jnp.int32, (tq, tq), 0)
    col = lax.broadcasted_iota(jnp.int32, (tq, tq), 1)

    def tile(j, carry, near, on=True):
        off = pl.multiple_of(j * tq, tq)
        s = _mmd_nt(q, k_ref[pl.ds(off, tq), :])
        expand = (blk_r == blk_c + j * (tq // SEL_BLOCK)).astype(_MM)
        picked = jnp.dot(selm, expand, preferred_element_type=F32) > 0.5
        if near:
            rel = (i - j) * tq + row - col
            s = jnp.where(picked & (rel >= 0) & on, s + _t5_bias(tbl, rel), -jnp.inf)
        else:
            s = jnp.where(picked, s + far_bias, -jnp.inf)
        return _online_update(carry, s, v_ref[pl.ds(off, tq), :])

    carry = lax.fori_loop(0, jnp.maximum(i - 1, 0), lambda j, c: tile(j, c, False), _flash_init(tq, HEAD_DIM))
    carry = tile(jnp.maximum(i - 1, 0), carry, True, i >= 1)
    _, l, acc = tile(i, carry, True)
    o_ref[...] = acc / l


def _nsa_sel_prompt(p0, selmask, ftab, t, tq):
    n_blk = selmask.shape[2]
    return pl.pallas_call(
        functools.partial(_nsa_sel_kernel, tq=tq), grid=(NSA_HEADS, t // tq),
        in_specs=[pl.BlockSpec((1, 1, LANES), lambda h, i: (h, 0, 0)),
                  pl.BlockSpec((tq, HEAD_DIM), lambda h, i: (i, E_NQ + h)),
                  pl.BlockSpec((t, HEAD_DIM), lambda h, i: (0, E_SK + h // NSA_GROUP)),
                  pl.BlockSpec((t, HEAD_DIM), lambda h, i: (0, E_SV + h // NSA_GROUP)),
                  pl.BlockSpec((1, tq, n_blk), lambda h, i: (h // NSA_GROUP, i, 0))],
        out_specs=pl.BlockSpec((tq, HEAD_DIM), lambda h, i: (i, h)),
        out_shape=jax.ShapeDtypeStruct((t, NSA_HEADS * HEAD_DIM), F32),
        compiler_params=_cparams(2), name="nsa_sel")(ftab, p0, p0, p0, selmask)


def _nsa_win_kernel(f_ref, q_ref, kp_ref, kc_ref, vp_ref, vc_ref, g_ref, ocmp_ref, osel_ref, o_ref, *, tq):
    h = pl.program_id(0)
    i = pl.program_id(1)
    q = q_ref[...] * SCALE
    tbl = jnp.broadcast_to(f_ref[0], (tq, LANES))
    row = lax.broadcasted_iota(jnp.int32, (tq, tq), 0)
    col = lax.broadcasted_iota(jnp.int32, (tq, tq), 1)
    rel_prev = tq + row - col
    s = _mmd_nt(q, kp_ref[...])
    s = jnp.where((rel_prev < WINDOW) & (i >= 1), s + _t5_bias(tbl, rel_prev), -jnp.inf)
    carry = _online_update(_flash_init(tq, HEAD_DIM), s, vp_ref[...])
    rel = row - col
    s = _mmd_nt(q, kc_ref[...])
    s = jnp.where((rel >= 0) & (rel < WINDOW), s + _t5_bias(tbl, rel), -jnp.inf)
    _, l, acc = _online_update(carry, s, vc_ref[...])
    gates = jax.nn.sigmoid(g_ref[...])
    lane = lax.broadcasted_iota(jnp.int32, (tq, LANES), 1)
    gate = lambda k: jnp.sum(jnp.where(lane == 3 * h + k, gates, 0.0), axis=-1, keepdims=True)
    o_ref[...] = gate(0) * ocmp_ref[...] + gate(1) * osel_ref[...] + gate(2) * (acc / l)


def _nsa_win_prompt(p0, o_cmp, o_sel, ftab, t, tq):
    assert tq == WINDOW
    prev = lambda i: jnp.maximum(i - 1, 0)
    grp = lambda h: h // NSA_GROUP
    blk = lambda f: pl.BlockSpec((tq, HEAD_DIM), f)
    return pl.pallas_call(
        functools.partial(_nsa_win_kernel, tq=tq), grid=(NSA_HEADS, t // tq),
        in_specs=[pl.BlockSpec((1, 1, LANES), lambda h, i: (h, 0, 0)),
                  blk(lambda h, i: (i, E_NQ + h)),
                  blk(lambda h, i: (prev(i), E_WK + grp(h))), blk(lambda h, i: (i, E_WK + grp(h))),
                  blk(lambda h, i: (prev(i), E_WV + grp(h))), blk(lambda h, i: (i, E_WV + grp(h))),
                  blk(lambda h, i: (i, E_G)), blk(lambda h, i: (i, h)), blk(lambda h, i: (i, h))],
        out_specs=blk(lambda h, i: (i, h)),
        out_shape=jax.ShapeDtypeStruct((t, NSA_HEADS * HEAD_DIM), F32),
        compiler_params=_cparams(2), name="nsa_win")(ftab, p0, p0, p0, p0, p0, p0, o_cmp, o_sel)


def _diff_lambda(dl, lam_init):
    return (jnp.exp(jnp.sum(dl[0:1] * dl[1:2], axis=-1, keepdims=True))
            - jnp.exp(jnp.sum(dl[2:3] * dl[3:4], axis=-1, keepdims=True)) + lam_init)


def _diff_prompt_kernel(f_ref, q_ref, k_ref, v_ref, lam_ref, sub_ref, o_ref, *, tq, lam_init):
    i = pl.program_id(1)
    lane = lax.broadcasted_iota(jnp.int32, (tq, HEAD_DIM), 1)
    qf = q_ref[...] * DIFF_SCALE
    q12 = jnp.concatenate([jnp.where(lane < DIFF_QK, qf, 0.0), jnp.where(lane >= DIFF_QK, qf, 0.0)], axis=0)
    tbl = jnp.broadcast_to(f_ref[0], (tq, LANES))
    far_bias = f_ref[0][:, LANES - 1:LANES]
    row = lax.broadcasted_iota(jnp.int32, (tq, tq), 0)
    col = lax.broadcasted_iota(jnp.int32, (tq, tq), 1)

    def tile(j, carry, near, on=True):
        off = pl.multiple_of(j * tq, tq)
        s = _mmd_nt(q12, k_ref[pl.ds(off, tq), :])
        if near:
            rel = (i - j) * tq + row - col
            bias = jnp.where((rel >= 0) & on, _t5_bias(tbl, rel), -jnp.inf)
            s = s + jnp.concatenate([bias, bias], axis=0)
        else:
            s = s + far_bias
        return _online_update(carry, s, v_ref[pl.ds(off, tq), :])

    carry = lax.fori_loop(0, jnp.maximum(i - 1, 0), lambda j, c: tile(j, c, False), _flash_init(2 * tq, HEAD_DIM))
    carry = tile(jnp.maximum(i - 1, 0), carry, True, i >= 1)
    _, l, acc = tile(i, carry, True)
    o12 = acc / l
    lam = _diff_lambda(lam_ref[...], lam_init)
    o = o12[:tq] - lam * o12[tq:]
    o = o * lax.rsqrt(jnp.mean(o * o, axis=-1, keepdims=True) + LN_EPS) * sub_ref[...] * (1.0 - lam_init)
    o_ref[...] = o


def _diff_prompt(p0, ftab, lam4, subln, t, tq, lam_init):
    return pl.pallas_call(
        functools.partial(_diff_prompt_kernel, tq=tq, lam_init=lam_init), grid=(DIFF_HEADS, t // tq),
        in_specs=[pl.BlockSpec((1, 1, LANES), lambda h, i: (NSA_HEADS + h, 0, 0)),
                  pl.BlockSpec((tq, HEAD_DIM), lambda h, i: (i, E_DQ + h)),
                  pl.BlockSpec((t, HEAD_DIM), lambda h, i: (0, E_DK + h)),
                  pl.BlockSpec((t, HEAD_DIM), lambda h, i: (0, E_DV + h)),
                  pl.BlockSpec((4, DIFF_QK), lambda h, i: (0, 0)),
                  pl.BlockSpec((1, HEAD_DIM), lambda h, i: (0, 0))],
        out_specs=pl.BlockSpec((tq, HEAD_DIM), lambda h, i: (i, h)),
        out_shape=jax.ShapeDtypeStruct((t, DIFF_HEADS * HEAD_DIM), F32),
        compiler_params=_cparams(2), name="diff_prompt")(ftab, p0, p0, p0, lam4, subln)


def _log_sigmoid(x):
    return jnp.minimum(x, 0.0) - jnp.log(1.0 + jnp.exp(-jnp.abs(x)))


def _fox_prep_kernel(f_ref, b_ref, lf_ref, cum_ref, carry_ref, *, tm):
    @pl.when(pl.program_id(0) == 0)
    def _():
        carry_ref[...] = jnp.zeros(carry_ref.shape, F32)

    lf = _log_sigmoid(f_ref[...] + b_ref[...])
    lf_ref[...] = lf
    r = lax.broadcasted_iota(jnp.int32, (tm, tm), 0)
    c = lax.broadcasted_iota(jnp.int32, (tm, tm), 1)
    cum = _mmd_exact_lhs01((r >= c).astype(_MM), lf) + carry_ref[...]
    cum_ref[...] = cum
    carry_ref[...] = cum[tm - 1:tm, :]


def _fox_prep(p1, b_f, tm):
    m = p1.shape[0]
    b = jnp.pad(b_f.reshape(1, FOX_HEADS), ((0, 0), (0, LANES - FOX_HEADS)))
    return pl.pallas_call(
        functools.partial(_fox_prep_kernel, tm=tm), grid=(m // tm,),
        in_specs=[pl.BlockSpec((tm, LANES), lambda i: (i, O_F)), pl.BlockSpec((1, LANES), lambda i: (0, 0))],
        out_specs=[pl.BlockSpec((tm, LANES), lambda i: (i, 0)), pl.BlockSpec((tm, LANES), lambda i: (i, 0))],
        out_shape=[jax.ShapeDtypeStruct((m, LANES), F32), jax.ShapeDtypeStruct((m, LANES), F32)],
        scratch_shapes=[pltpu.VMEM((1, LANES), F32)], compiler_params=_cparams(1), name="fox_prep")(p1, b)


def _fox_prompt_kernel(q_ref, k_ref, v_ref, cum_ref, ck_ref, o_ref, *, tq):
    h = pl.program_id(0)
    i = pl.program_id(1)
    q = q_ref[...] * SCALE
    lane = lax.broadcasted_iota(jnp.int32, (tq, LANES), 1)
    cq = jnp.sum(jnp.where(lane == h, cum_ref[...], 0.0), axis=-1, keepdims=True)
    row = lax.broadcasted_iota(jnp.int32, (tq, tq), 0)
    col = lax.broadcasted_iota(jnp.int32, (tq, tq), 1)

    def tile(j, carry, diag):
        off = pl.multiple_of(j * tq, tq)
        s = (_mmd_nt(q, k_ref[pl.ds(off, tq), :]) + cq) - ck_ref[0, j]
        if diag:
            s = jnp.where(row >= col, s, -jnp.inf)
        return _online_update(carry, s, v_ref[pl.ds(off, tq), :])

    carry = lax.fori_loop(0, i, lambda j, c: tile(j, c, False), _flash_init(tq, HEAD_DIM))
    _, l, acc = tile(i, carry, True)
    o_ref[...] = acc / l


def _fox_prompt(p1, cum, t, tq):
    ck = cum[:, :FOX_HEADS].T.reshape(FOX_HEADS, t // tq, 1, tq)
    return pl.pallas_call(
        functools.partial(_fox_prompt_kernel, tq=tq), grid=(FOX_HEADS, t // tq),
        in_specs=[pl.BlockSpec((tq, HEAD_DIM), lambda h, i: (i, O_Q + h)),
                  pl.BlockSpec((t, HEAD_DIM), lambda h, i: (0, O_K + h)),
                  pl.BlockSpec((t, HEAD_DIM), lambda h, i: (0, O_V + h)),
                  pl.BlockSpec((tq, LANES), lambda h, i: (i, 0)),
                  pl.BlockSpec((1, t // tq, 1, tq), lambda h, i: (h, 0, 0, 0))],
        out_specs=pl.BlockSpec((tq, HEAD_DIM), lambda h, i: (i, h)),
        out_shape=jax.ShapeDtypeStruct((t, FOX_HEADS * HEAD_DIM), F32),
        compiler_params=_cparams(2), name="fox_prompt")(p1, p1, p1, cum, ck)


def _mem_attn_kernel(q_ref, kv_ref, o_ref):
    q = q_ref[0] * SCALE
    rows = q.shape[0]
    if rows < 8:
        q = jnp.broadcast_to(q[0:1], (8, MEM_WIDTH))
    for h in range(MEM_HEADS):
        sl = slice(h * HEAD_DIM, (h + 1) * HEAD_DIM)
        s = _mmd_nt(q[:, sl], kv_ref[0, :, sl])
        e = jnp.exp(s - jnp.max(s, axis=-1, keepdims=True))
        p = e / jnp.sum(e, axis=-1, keepdims=True)
        o = _mmd(p, kv_ref[0, :, MEM_WIDTH + h * HEAD_DIM:MEM_WIDTH + (h + 1) * HEAD_DIM])
        o_ref[0, :, sl] = o[:rows]


def _mem_attn(q, kv, tq):
    b, rows, _ = q.shape
    return pl.pallas_call(
        _mem_attn_kernel, grid=(b, rows // tq),
        in_specs=[pl.BlockSpec((1, tq, MEM_WIDTH), lambda b, i: (b, i, 0)),
                  pl.BlockSpec((1, N_MEM, 2 * MEM_WIDTH), lambda b, i: (b, 0, 0))],
        out_specs=pl.BlockSpec((1, tq, MEM_WIDTH), lambda b, i: (b, i, 0)),
        out_shape=jax.ShapeDtypeStruct((b, rows, MEM_WIDTH), F32), compiler_params=_cparams(2),
        name="mem_attn")(q, kv)


def _first_max(x, axis, n):
    idx = lax.broadcasted_iota(jnp.int32, x.shape, axis)
    m = jnp.max(x, axis=axis, keepdims=True)
    first = jnp.min(jnp.where(x == m, idx, n), axis=axis, keepdims=True)
    return m, idx == first, first


def _router_kernel(x_ref, w_ref, b_ref, e_ref, g_ref):
    tm = x_ref.shape[0]
    per = N_EXPERTS // N_EXPERT_GROUPS
    aff = jax.nn.sigmoid(_mmd_nt_precise(w_ref[...], x_ref[...]))
    biased = aff + b_ref[...]
    b3 = biased.reshape(N_EXPERT_GROUPS, per, tm)
    m1, pick1, _ = _first_max(b3, 1, per)
    m2 = jnp.max(jnp.where(pick1, -jnp.inf, b3), axis=1, keepdims=True)
    gs = (m1 + m2).reshape(N_EXPERT_GROUPS, tm)
    grp_on = jnp.zeros((N_EXPERT_GROUPS, tm), F32)
    for _ in range(TOPK_GROUPS):
        _, pick, _ = _first_max(gs, 0, N_EXPERT_GROUPS)
        grp_on = jnp.where(pick, 1.0, grp_on)
        gs = jnp.where(pick, -jnp.inf, gs)
    on = jnp.broadcast_to(grp_on.reshape(N_EXPERT_GROUPS, 1, tm), (N_EXPERT_GROUPS, per, tm)).reshape(N_EXPERTS, tm)
    cand = jnp.where(on > 0.5, biased, MASKED)
    gates, idxs = [], []
    for _ in range(TOP_K):
        _, pick, first = _first_max(cand, 0, N_EXPERTS)
        gates.append(jnp.sum(jnp.where(pick, aff, 0.0), axis=0, keepdims=True))
        idxs.append(first)
        cand = jnp.where(pick, -jnp.inf, cand)
    gate = jnp.concatenate(gates, axis=0)
    g_ref[...] = gate / jnp.sum(gate, axis=0, keepdims=True) * ROUTE_SCALE
    e_ref[...] = jnp.concatenate(idxs, axis=0)


def _router(x, w_t, b, tm):
    n = x.shape[0]
    return pl.pallas_call(
        _router_kernel, grid=(n // tm,),
        in_specs=[pl.BlockSpec((tm, D_MODEL), lambda i: (i, 0)), pl.BlockSpec((N_EXPERTS, D_MODEL), lambda i: (0, 0)),
                  pl.BlockSpec((N_EXPERTS, 1), lambda i: (0, 0))],
        out_specs=[pl.BlockSpec((TOP_K, tm), lambda i: (0, i)), pl.BlockSpec((TOP_K, tm), lambda i: (0, i))],
        out_shape=[jax.ShapeDtypeStruct((TOP_K, n), jnp.int32), jax.ShapeDtypeStruct((TOP_K, n), F32)],
        compiler_params=_cparams(1), name="router")(x, w_t, b.reshape(N_EXPERTS, 1))


def _ffn_kernel(be_ref, nv_ref, x_ref, wg_ref, wu_ref, wd_ref, rg_ref, o_ref):
    i = pl.program_id(0)

    @pl.when(i < nv_ref[0])
    def _():
        x = x_ref[...]
        h = jax.nn.silu(_mmd(x, wg_ref[0])) * _mmd(x, wu_ref[0])
        o_ref[...] = _mmd(h, wd_ref[0]) * rg_ref[...]

    @pl.when(i >= nv_ref[0])
    def _():
        o_ref[...] = jnp.zeros(o_ref.shape, F32)


def _ffn(xs, w_gate, w_up, w_down, blk_expert, n_valid, row_gate, blk):
    n_rows, d = xs.shape
    f = w_gate.shape[2]
    live = lambda i, nv: jnp.minimum(i, nv[0] - 1)
    grid_spec = pltpu.PrefetchScalarGridSpec(
        num_scalar_prefetch=2, grid=(n_rows // blk,),
        in_specs=[pl.BlockSpec((blk, d), lambda i, be, nv: (live(i, nv), 0)),
                  pl.BlockSpec((1, d, f), lambda i, be, nv: (be[live(i, nv)], 0, 0)),
                  pl.BlockSpec((1, d, f), lambda i, be, nv: (be[live(i, nv)], 0, 0)),
                  pl.BlockSpec((1, f, d), lambda i, be, nv: (be[live(i, nv)], 0, 0)),
                  pl.BlockSpec((blk, 1), lambda i, be, nv: (live(i, nv), 0))],
        out_specs=pl.BlockSpec((blk, d), lambda i, be, nv: (i, 0)))
    return pl.pallas_call(
        _ffn_kernel, grid_spec=grid_spec, out_shape=jax.ShapeDtypeStruct((n_rows, d), F32),
        compiler_params=_cparams(1), name="ffn")(blk_expert, n_valid, xs, w_gate, w_up, w_down, row_gate)


def _moe_out_kernel(x_ref, y_ref, sh_ref, g_ref, b_ref, o_ref):
    routed = y_ref[0]
    for k in range(1, TOP_K):
        routed = routed + y_ref[k]
    o_ref[...] = _layer_norm(DEEPNORM_ALPHA * x_ref[...] + (routed + sh_ref[...]), g_ref[...], b_ref[...])


def _moe_out(x, y8, y_block0, shared, g, b, tm):
    n, d = x.shape
    return pl.pallas_call(
        _moe_out_kernel, grid=(n // tm,),
        in_specs=[pl.BlockSpec((tm, d), lambda i: (i, 0)), pl.BlockSpec((TOP_K, tm, d), lambda i: (0, y_block0 + i, 0)),
                  pl.BlockSpec((tm, d), lambda i: (i, 0)), pl.BlockSpec((1, d), lambda i: (0, 0)),
                  pl.BlockSpec((1, d), lambda i: (0, 0))],
        out_specs=pl.BlockSpec((tm, d), lambda i: (i, 0)), out_shape=jax.ShapeDtypeStruct((n, d), F32),
        compiler_params=_cparams(1), name="moe_out")(x, y8, shared, g.reshape(1, d), b.reshape(1, d))


def _moe_plan(eidx, gate, n, blk):
    nk = n * TOP_K
    flat_e = eidx.reshape(-1)
    order = jnp.argsort(flat_e, stable=True).astype(jnp.int32)
    sorted_e = flat_e[order]
    counts = jnp.sum((flat_e[:, None] == jnp.arange(N_EXPERTS)[None, :]).astype(jnp.int32), axis=0)
    padded = (counts + blk - 1) // blk * blk
    pad_end = jnp.cumsum(padded)
    pad_start = pad_end - padded
    start = jnp.cumsum(counts) - counts
    n_blocks = -(-(nk + N_EXPERTS * (blk - 1)) // blk)
    first_row = jnp.arange(n_blocks) * blk
    blk_expert = jnp.minimum(jnp.sum((pad_end[None, :] <= first_row[:, None]).astype(jnp.int32), axis=1),
                             N_EXPERTS - 1).astype(jnp.int32)
    r = jnp.arange(n_blocks * blk)
    e_r = blk_expert[r // blk]
    off = r - pad_start[e_r]
    valid = (off >= 0) & (off < counts[e_r])
    src = order[jnp.clip(start[e_r] + off, 0, nk - 1)]
    row_tok = jnp.where(valid, src // TOP_K, n).astype(jnp.int32)
    row_gate = jnp.where(valid, gate.reshape(-1)[src], 0.0)
    dest_sorted = (pad_start[sorted_e] + jnp.arange(nk) - start[sorted_e]).astype(jnp.int32)
    dest = jnp.zeros((nk,), jnp.int32).at[order].set(dest_sorted)
    n_valid = (pad_end[-1] // blk).astype(jnp.int32).reshape(1)
    return row_tok, row_gate.reshape(-1, 1), blk_expert, n_valid, dest


def _bucket_lut():
    n = np.arange(LANES)
    nf = np.maximum(n, 1).astype(np.float32)
    exact = N_BUCKETS // 2
    large = exact + (np.log(nf / exact) / np.float32(math.log(MAX_DISTANCE / exact)) * (N_BUCKETS - exact)).astype(np.int32)
    return np.where(n < exact, n, np.minimum(large, N_BUCKETS - 1))


def _t5_table(rel_bias):
    return rel_bias[_bucket_lut()].T


def _even_params(w_in, cmp_w, cmp_pos):
    d = w_in.shape[0]
    n_g = 3 * NSA_HEADS
    g0 = (E_WV + NSA_KV) * LANES
    w = jnp.concatenate([w_in[:, :g0], w_in[:, g0 + n_g:], w_in[:, g0:g0 + n_g],
                         jnp.zeros((d, LANES - n_g), w_in.dtype)], axis=1).astype(_MM)
    half = CMP_STRIDE * HEAD_DIM
    top = cmp_w[:, :, :CMP_STRIDE].reshape(2, NSA_KV, half, HEAD_DIM)
    bot = cmp_w[:, :, CMP_STRIDE:].reshape(2, NSA_KV, half, HEAD_DIM)
    wcat = jnp.concatenate([top, bot], axis=-1).reshape(2 * NSA_KV, half, 2 * HEAD_DIM).astype(_MM)
    pp = cmp_pos.transpose(0, 2, 1, 3)
    ptop = pp[:, :, :CMP_STRIDE].reshape(2 * NSA_KV, 1, half)
    pbot = pp[:, :, CMP_STRIDE:].reshape(2 * NSA_KV, 1, half)
    pos2 = jnp.concatenate([ptop, pbot, jnp.zeros((2 * NSA_KV, 6, half), F32)], axis=1)
    return w, wcat, pos2


def _even_attend_prompt(p0, wcat, pos2, ftab, lam4, subln, lam_init, t, tq):
    h4 = p0[:, E_CK * LANES:E_SK * LANES].reshape(t, 2 * NSA_KV, HEAD_DIM).transpose(1, 0, 2)
    h4 = h4.reshape(2 * NSA_KV, t // CMP_STRIDE, CMP_STRIDE * HEAD_DIM)
    kvc = _compress_prompt(h4, wcat, pos2)
    ftab3 = ftab.reshape(NSA_HEADS + DIFF_HEADS, 1, LANES)
    o_cmp, selmask = _nsa_cmp_prompt(p0, kvc, ftab[:NSA_HEADS].reshape(NSA_KV, NSA_GROUP, LANES), t, min(tq, 256))
    o_sel = _nsa_sel_prompt(p0, selmask, ftab3, t, tq)
    o_nsa = _nsa_win_prompt(p0, o_cmp, o_sel, ftab3, t, tq)
    o_diff = _diff_prompt(p0, ftab3, lam4, subln.reshape(1, HEAD_DIM), t, tq, lam_init)
    return o_nsa, o_diff


def _rows_of_group(a0, a1):
    row = lax.broadcasted_iota(jnp.int32, a0.shape, 0)
    return jnp.where(row < NSA_GROUP, a0, a1)


def _softmax_with_new(s, s_new):
    m = jnp.maximum(jnp.max(s, axis=-1, keepdims=True), s_new)
    p = jnp.exp(s - m)
    p_new = jnp.exp(s_new - m)
    inv = 1.0 / (jnp.sum(p, axis=-1, keepdims=True) + p_new)
    return p * inv, p_new * inv


def _nsa_sample_kernel(pt_ref, *refs, pp, n_pages):
    page_refs = refs[:pp]
    (q_ref, new_ref, wnew_ref, wbuf_ref, g_ref, wcat_ref, pos2_ref, bcmp_ref, bsel_ref, bwin_ref, f0_ref,
     o_ref, hc_ref, ks_ref, vs_ref) = refs[pp:]
    j = pl.program_id(1)
    halves = PAGE_SIZE // CMP_STRIDE
    n_slot = 4 * NSA_KV
    for k in range(pp):
        pg = j * pp + k
        for s in range(2 * NSA_KV):
            for l in range(CMP_STRIDE):
                piece = page_refs[k][0, pl.ds(l * n_slot + s, halves, stride=CMP_STRIDE * n_slot), :]
                hc_ref[s, pl.ds(pl.multiple_of(pg * halves, SUBLANES), halves), l * HEAD_DIM:(l + 1) * HEAD_DIM] = piece
        rows = pl.ds(pl.multiple_of(pg * PAGE_SIZE, PAGE_SIZE), PAGE_SIZE)
        for g in range(NSA_KV):
            ks_ref[g, rows, :] = page_refs[k][0, pl.ds(2 * NSA_KV + g, PAGE_SIZE, stride=n_slot), :]
            vs_ref[g, rows, :] = page_refs[k][0, pl.ds(3 * NSA_KV + g, PAGE_SIZE, stride=n_slot), :]

    @pl.when(j == pl.num_programs(1) - 1)
    def _():
        q = q_ref[0] * SCALE
        f0 = f0_ref[...]
        comp = [_compress(hc_ref[s], wcat_ref[s], pos2_ref[s]) for s in range(2 * NSA_KV)]
        lg = _rows_of_group(_mmd_nt(q, comp[0]), _mmd_nt(q, comp[1])) + bcmp_ref[...]
        e = jnp.exp(lg - jnp.max(lg, axis=-1, keepdims=True))
        p = e / jnp.sum(e, axis=-1, keepdims=True)
        o_cmp = _rows_of_group(_mmd(p, comp[2]), _mmd(p, comp[3]))
        n_blk = 2 * n_pages
        sc = _mmd_exact_rhs01(p, _sel_score_matrix(p.shape[1], n_blk))
        sc0 = jnp.sum(sc[:NSA_GROUP], axis=0, keepdims=True)
        sc1 = jnp.sum(sc[NSA_GROUP:], axis=0, keepdims=True)
        score = _rows_of_group(jnp.broadcast_to(sc0, sc.shape), jnp.broadcast_to(sc1, sc.shape))
        lane = lax.broadcasted_iota(jnp.int32, score.shape, 1)
        selm = _top_blocks(jnp.where(lane == 0, 1e9, score), N_SEL - 1).astype(_MM)
        ch = 4 * PAGE_SIZE
        n_ch = n_pages * PAGE_SIZE // ch
        eb = lax.broadcasted_iota(jnp.int32, (n_blk, ch), 0)
        blk_of_key = lax.broadcasted_iota(jnp.int32, (n_blk, ch), 1) // SEL_BLOCK
        s_parts = []
        for c in range(n_ch):
            sl = slice(c * ch, (c + 1) * ch)
            s = _rows_of_group(_mmd_nt(q, ks_ref[0, sl, :]), _mmd_nt(q, ks_ref[1, sl, :]))
            expand = (eb == blk_of_key + (ch // SEL_BLOCK) * c).astype(_MM)
            picked = jnp.dot(selm, expand, preferred_element_type=F32) > 0.5
            s_parts.append(jnp.where(picked, s + bsel_ref[:, sl], -jnp.inf))
        s_all = jnp.concatenate(s_parts, axis=1)
        new = new_ref[0]
        hd = HEAD_DIM
        k_new = _rows_of_group(jnp.broadcast_to(new[:, 4 * hd:5 * hd], q.shape), jnp.broadcast_to(new[:, 5 * hd:6 * hd], q.shape))
        v_new = _rows_of_group(jnp.broadcast_to(new[:, 6 * hd:7 * hd], q.shape), jnp.broadcast_to(new[:, 7 * hd:8 * hd], q.shape))
        p, p_new = _softmax_with_new(s_all, jnp.sum(q * k_new, axis=-1, keepdims=True) + f0)
        o_sel = p_new * v_new
        for c in range(n_ch):
            sl = slice(c * ch, (c + 1) * ch)
            o_sel = o_sel + _rows_of_group(_mmd(p[:, sl], vs_ref[0, sl, :]), _mmd(p[:, sl], vs_ref[1, sl, :]))
        s = _rows_of_group(_mmd_nt(q, wbuf_ref[0, :, 0:hd]), _mmd_nt(q, wbuf_ref[0, :, hd:2 * hd])) + bwin_ref[...]
        wnew = wnew_ref[0]
        k_new = _rows_of_group(jnp.broadcast_to(wnew[:, 0:hd], q.shape), jnp.broadcast_to(wnew[:, hd:2 * hd], q.shape))
        v_new = _rows_of_group(jnp.broadcast_to(wnew[:, 2 * hd:3 * hd], q.shape), jnp.broadcast_to(wnew[:, 3 * hd:4 * hd], q.shape))
        p, p_new = _softmax_with_new(s, jnp.sum(q * k_new, axis=-1, keepdims=True) + f0)
        o_win = p_new * v_new + _rows_of_group(_mmd(p, wbuf_ref[0, :, 2 * hd:3 * hd]), _mmd(p, wbuf_ref[0, :, 3 * hd:4 * hd]))
        gates = jnp.broadcast_to(jax.nn.sigmoid(g_ref[0]), (NSA_HEADS, LANES))
        lane = lax.broadcasted_iota(jnp.int32, (NSA_HEADS, LANES), 1)
        row = lax.broadcasted_iota(jnp.int32, (NSA_HEADS, LANES), 0)
        gate = lambda k: jnp.sum(jnp.where(lane == 3 * row + k, gates, 0.0), axis=-1, keepdims=True)
        o_ref[0] = gate(0) * o_cmp + gate(1) * o_sel + gate(2) * o_win


def _nsa_sample(pool, page_table, q, new_rows, win_new, win_buf, g_raw, wcat, pos2, ftab, pp):
    b, n_pages = page_table.shape
    t_past = n_pages * PAGE_SIZE
    halves = PAGE_SIZE // CMP_STRIDE
    n_c = n_pages * halves
    assert win_buf.shape[1] == WINDOW and n_pages % pp == 0 and t_past % (4 * PAGE_SIZE) == 0
    f = ftab[:NSA_HEADS]
    ninf = jnp.float32(-jnp.inf)
    c = np.arange(n_c)
    rel_c = t_past - (c * CMP_STRIDE + CMP_LEN - 1)
    bcmp = jnp.where((rel_c >= 0) & (c < n_c - 1), f[:, np.clip(rel_c, 0, LANES - 1)], ninf)
    bsel = f[:, np.clip(t_past - np.arange(t_past), 0, LANES - 1)]
    wi = np.arange(WINDOW)
    bwin = jnp.where(wi >= 1, f[:, np.clip(WINDOW - wi, 0, LANES - 1)], ninf)
    f0 = f[:, 0:1]
    page_spec = lambda k: pl.BlockSpec((1,) + pool.shape[1:], lambda s, j, pt: (pt[s, j * pp + k], 0, 0))
    full = lambda a: pl.BlockSpec(a.shape, lambda s, j, pt: (0,) * a.ndim)
    per_seq = lambda a: pl.BlockSpec((1,) + a.shape[1:], lambda s, j, pt: (s,) + (0,) * (a.ndim - 1))
    ops = [q, new_rows, win_new, win_buf, g_raw]
    consts = [wcat, pos2, bcmp, bsel, bwin, f0]
    grid_spec = pltpu.PrefetchScalarGridSpec(
        num_scalar_prefetch=1, grid=(b, n_pages // pp),
        in_specs=[page_spec(k) for k in range(pp)] + [per_seq(a) for a in ops] + [full(a) for a in consts],
        out_specs=pl.BlockSpec((1, NSA_HEADS, HEAD_DIM), lambda s, j, pt: (s, 0, 0)),
        scratch_shapes=[pltpu.VMEM((2 * NSA_KV, n_c, CMP_STRIDE * HEAD_DIM), F32),
                        pltpu.VMEM((NSA_KV, t_past, HEAD_DIM), F32), pltpu.VMEM((NSA_KV, t_past, HEAD_DIM), F32)])
    return pl.pallas_call(
        functools.partial(_nsa_sample_kernel, pp=pp, n_pages=n_pages), grid_spec=grid_spec,
        out_shape=jax.ShapeDtypeStruct((b, NSA_HEADS, HEAD_DIM), F32), compiler_params=_cparams(2),
        name="nsa_sample")(page_table, *([pool] * pp), *ops, *consts)


def _page_attend(page_ref, q_ref, n_heads, rows_per_head, scale, bias, m_ref, l_ref, acc_ref):
    stride = 2 * n_heads
    per_tile = SUBLANES // rows_per_head
    n_tiles = n_heads // per_tile
    row = lax.broadcasted_iota(jnp.int32, (SUBLANES, LANES), 0)
    tiles = []
    for tile in range(n_tiles):
        s = None
        for r in range(per_tile):
            h = tile * per_tile + r
            kh = page_ref[0, pl.ds(h, PAGE_SIZE, stride=stride), :]
            d = _mmd_nt(q_ref[0, h * SUBLANES:(h + 1) * SUBLANES, :] * scale, kh)
            s = d if s is None else s + d
        tiles.append(s)
    s = jnp.concatenate(tiles, axis=0) + bias
    m_old = m_ref[...]
    m_new = jnp.maximum(m_old, jnp.max(s, axis=-1, keepdims=True))
    alpha = jnp.exp(m_old - m_new)
    p = jnp.exp(s - m_new)
    l_ref[...] = alpha * l_ref[...] + jnp.sum(p, axis=-1, keepdims=True)
    m_ref[...] = m_new
    tiles = []
    for tile in range(n_tiles):
        pt = p[tile * SUBLANES:(tile + 1) * SUBLANES]
        o = None
        for r in range(per_tile):
            h = tile * per_tile + r
            vh = page_ref[0, pl.ds(n_heads + h, PAGE_SIZE, stride=stride), :]
            d = _mmd(jnp.where(row // rows_per_head == r, pt, 0.0), vh)
            o = d if o is None else o + d
        tiles.append(o)
    acc_ref[...] = alpha * acc_ref[...] + jnp.concatenate(tiles, axis=0)


def _query_groups(q_rows, n_heads, rows_per_head):
    b = q_rows.shape[0]
    per_tile = SUBLANES // rows_per_head
    q = q_rows.reshape(b, n_heads, 1, rows_per_head, HEAD_DIM)
    slot = (jnp.arange(n_heads) % per_tile)[:, None] == jnp.arange(per_tile)[None, :]
    q = jnp.where(slot[None, :, :, None, None], q, 0.0)
    return q.reshape(b, n_heads * SUBLANES, HEAD_DIM)


def _diff_sample_kernel(pt_ref, *refs, pp, n_pages, lam_init):
    page_refs = refs[:pp]
    (q_ref, qn_ref, kn_ref, vn_ref, blast_ref, fc_ref, f0_ref, lam_ref, sub_ref, o_ref,
     m_ref, l_ref, acc_ref) = refs[pp:]
    j = pl.program_id(1)

    @pl.when(j == 0)
    def _():
        m_ref[...] = jnp.sum(qn_ref[0] * DIFF_SCALE * kn_ref[0], axis=-1, keepdims=True) + f0_ref[...]
        l_ref[...] = jnp.ones(l_ref.shape, F32)
        acc_ref[...] = vn_ref[0]

    for k in range(pp):
        bias = jnp.where(j * pp + k == n_pages - 1, blast_ref[...], fc_ref[...])
        _page_attend(page_refs[k], q_ref, DIFF_HEADS, 2, DIFF_SCALE, bias, m_ref, l_ref, acc_ref)

    @pl.when(j == pl.num_programs(1) - 1)
    def _():
        branch = lambda ref, r: ref[pl.ds(r, DIFF_HEADS, stride=2), :]
        lam = _diff_lambda(lam_ref[...], lam_init)
        o = branch(acc_ref, 0) / branch(l_ref, 0) - lam * (branch(acc_ref, 1) / branch(l_ref, 1))
        o = o * lax.rsqrt(jnp.mean(o * o, axis=-1, keepdims=True) + LN_EPS) * sub_ref[...] * (1.0 - lam_init)
        o_ref[0] = o


def _diff_sample(pool, page_table, q, k_new, v_new, ftab, lam4, subln, lam_init, pp):
    b, n_pages = page_table.shape
    lo = jnp.arange(HEAD_DIM) < DIFF_QK
    qn = jnp.stack([jnp.where(lo, q, 0.0), jnp.where(lo, 0.0, q)], axis=2).reshape(b, 2 * DIFF_HEADS, HEAD_DIM)
    twice = lambda a: jnp.repeat(a, 2, axis=a.ndim - 2)
    f = twice(ftab[NSA_HEADS:])
    blast = f[:, np.clip(PAGE_SIZE - np.arange(PAGE_SIZE), 0, LANES - 1)]
    page_spec = lambda k: pl.BlockSpec((1,) + pool.shape[1:], lambda s, j, pt: (pt[s, j * pp + k], 0, 0))
    full = lambda a: pl.BlockSpec(a.shape, lambda s, j, pt: (0,) * a.ndim)
    per_seq = lambda a: pl.BlockSpec((1,) + a.shape[1:], lambda s, j, pt: (s,) + (0,) * (a.ndim - 1))
    ops = [_query_groups(qn, DIFF_HEADS, 2), qn, twice(k_new), twice(v_new)]
    consts = [blast, f[:, LANES - 1:LANES], f[:, 0:1], lam4, subln.reshape(1, HEAD_DIM)]
    rows = 2 * DIFF_HEADS
    grid_spec = pltpu.PrefetchScalarGridSpec(
        num_scalar_prefetch=1, grid=(b, n_pages // pp),
        in_specs=[page_spec(k) for k in range(pp)] + [per_seq(a) for a in ops] + [full(a) for a in consts],
        out_specs=pl.BlockSpec((1, DIFF_HEADS, HEAD_DIM), lambda s, j, pt: (s, 0, 0)),
        scratch_shapes=[pltpu.VMEM((rows, 1), F32), pltpu.VMEM((rows, 1), F32), pltpu.VMEM((rows, HEAD_DIM), F32)])
    return pl.pallas_call(
        functools.partial(_diff_sample_kernel, pp=pp, n_pages=n_pages, lam_init=lam_init), grid_spec=grid_spec,
        out_shape=jax.ShapeDtypeStruct((b, DIFF_HEADS, HEAD_DIM), F32), compiler_params=_cparams(2),
        name="diff_sample")(page_table, *([pool] * pp), *ops, *consts)


def _fox_sample_kernel(pt_ref, *refs, pp):
    kv_refs, lf_refs = refs[:pp], refs[pp:2 * pp]
    q_ref, qn_ref, kn_ref, vn_ref, lfn_ref, o_ref, m_ref, l_ref, acc_ref, suf_ref = refs[2 * pp:]
    j = pl.program_id(1)

    @pl.when(j == 0)
    def _():
        m_ref[...] = jnp.sum(qn_ref[0] * SCALE * kn_ref[0], axis=-1, keepdims=True)
        l_ref[...] = jnp.ones(l_ref.shape, F32)
        acc_ref[...] = vn_ref[0]
        suf_ref[...] = lfn_ref[0]

    r = lax.broadcasted_iota(jnp.int32, (PAGE_SIZE, PAGE_SIZE), 0)
    c = lax.broadcasted_iota(jnp.int32, (PAGE_SIZE, PAGE_SIZE), 1)
    later = (r > c).astype(_MM)
    for k in range(pp):
        lf = lf_refs[k][0]
        bias = _mmd_exact_rhs01(lf, later) + suf_ref[...]
        suf_ref[...] = suf_ref[...] + jnp.sum(lf, axis=-1, keepdims=True)
        _page_attend(kv_refs[k], q_ref, FOX_HEADS, 1, SCALE, bias, m_ref, l_ref, acc_ref)

    @pl.when(j == pl.num_programs(1) - 1)
    def _():
        o_ref[0] = acc_ref[...] / l_ref[...]


def _fox_sample(kv_pool, lf_pool_t, page_table, q, k_new, v_new, lf_new, pp):
    b, n_pages = page_table.shape
    back = lambda j, k: n_pages - 1 - (j * pp + k)
    kv_spec = lambda k: pl.BlockSpec((1,) + kv_pool.shape[1:], lambda s, j, pt: (pt[s, back(j, k)], 0, 0))
    lf_spec = lambda k: pl.BlockSpec((1, FOX_HEADS, PAGE_SIZE), lambda s, j, pt: (pt[s, back(j, k)], 0, 0))
    per_seq = lambda a: pl.BlockSpec((1,) + a.shape[1:], lambda s, j, pt: (s,) + (0,) * (a.ndim - 1))
    ops = [_query_groups(q, FOX_HEADS, 1), q, k_new, v_new, lf_new]
    grid_spec = pltpu.PrefetchScalarGridSpec(
        num_scalar_prefetch=1, grid=(b, n_pages // pp),
        in_specs=[kv_spec(k) for k in range(pp)] + [lf_spec(k) for k in range(pp)] + [per_seq(a) for a in ops],
        out_specs=pl.BlockSpec((1, FOX_HEADS, HEAD_DIM), lambda s, j, pt: (s, 0, 0)),
        scratch_shapes=[pltpu.VMEM((FOX_HEADS, 1), F32), pltpu.VMEM((FOX_HEADS, 1), F32),
                        pltpu.VMEM((FOX_HEADS, HEAD_DIM), F32), pltpu.VMEM((FOX_HEADS, 1), F32)])
    return pl.pallas_call(
        functools.partial(_fox_sample_kernel, pp=pp), grid_spec=grid_spec,
        out_shape=jax.ShapeDtypeStruct((b, FOX_HEADS, HEAD_DIM), F32), compiler_params=_cparams(2),
        name="fox_sample")(page_table, *([kv_pool] * pp), *([lf_pool_t] * pp), *ops)


TQ = 512
TM = 256
PAGES_PER_STEP = 4


def _memory_layer(hp, hs, mem_kv_p, mem_kv_s, w_q, w_o, g, b):
    bs = hs.shape[0]
    w_q = w_q.astype(_MM)
    w_o = w_o.astype(_MM)
    qp = _mm(hp, w_q, TQ, MEM_WIDTH)
    op = _mem_attn(qp[None], mem_kv_p[None], TQ)[0]
    hp = _mm_ln([op], [w_o], hp, g, b, TM)
    qs = _mm(hs, w_q, bs, MEM_WIDTH)
    os_ = _mem_attn(qs.reshape(bs, 1, MEM_WIDTH), mem_kv_s, 1).reshape(bs, MEM_WIDTH)
    hs = _mm_ln([os_], [w_o], hs, g, b, bs)
    return hp, hs


def _moe_layer(hp, hs, w_router, b_router, w_gate, w_up, w_down, ws_gate, ws_up, ws_down, g, b):
    t, d = hp.shape
    bs = hs.shape[0]
    n = t + bs
    assert t % bs == 0
    w_rt = w_router.T
    ep, gp = _router(hp, w_rt, b_router, TQ)
    es, gs = _router(hs, w_rt, b_router, bs)
    eidx = jnp.concatenate([ep, es], axis=1).T
    gate = jnp.concatenate([gp, gs], axis=1).T
    row_tok, row_gate, blk_expert, n_valid, dest = _moe_plan(eidx, gate, n, MOE_BLK)
    x_all = jnp.concatenate([hp, hs, jnp.zeros((1, d), F32)], axis=0).astype(_MM)
    xs = x_all[row_tok]
    ys = _ffn(xs, w_gate, w_up, w_down, blk_expert, n_valid, row_gate, MOE_BLK)
    y8 = ys[dest.reshape(n, TOP_K).T.reshape(-1)].reshape(TOP_K, n, d)
    one = lambda m, blk: (jnp.zeros((m // blk,), jnp.int32), jnp.full((1,), m // blk, jnp.int32), jnp.ones((m, 1), F32))
    sh_p = _ffn(hp, ws_gate[None], ws_up[None], ws_down[None], *one(t, TQ), TQ)
    sh_s = _ffn(hs, ws_gate[None], ws_up[None], ws_down[None], *one(bs, bs), bs)
    hp = _moe_out(hp, y8, 0, sh_p, g, b, LANES)
    hs = _moe_out(hs, y8, t // bs, sh_s, g, b, bs)
    return hp, hs


def kernel(x_prompt, x_sample, cache_nsa_kv, cache_nsa_win, cache_diff_kv, cache_fox_kv, cache_fox_logf, cache_mem_kv, page_table, mem_prompt, rel_bias, w_even_in, w_even_out, nsa_cmp_pos, nsa_cmp_w, diff_lambda, diff_subln, w_odd_in, b_forget, w_odd_out, w_mem_q, w_mem_kv, w_mem_o, ln_g, ln_b, w_router, b_router, w_exp_gate, w_exp_up, w_exp_down, w_sh_gate, w_sh_up, w_sh_down):
    bp, t, d = x_prompt.shape
    bs = x_sample.shape[0]
    assert bp == 1 and x_sample.shape[1] == 1 and t % TQ == 0
    n_pool = cache_nsa_kv.shape[1]
    wb = cache_nsa_win.shape[2]
    hp = x_prompt.reshape(t, d)
    hs = x_sample.reshape(bs, d)
    ftab = _t5_table(rel_bias)
    L = LANES
    p_nsa, p_win, p_diff, p_fox, p_logf, p_mem = [], [], [], [], [], []
    s_nsa, s_win, s_diff, s_fox, s_logf = [], [], [], [], []
    for layer in range(DEPTH):
        i = layer // 2
        if layer % 2 == 0:
            lam_init = 0.8 - 0.6 * math.exp(-0.3 * layer)
            w_in, wcat, pos2 = _even_params(w_even_in[i], nsa_cmp_w[i], nsa_cmp_pos[i])
            w_out = w_even_out[i].astype(_MM)
            nsa_w = NSA_HEADS * HEAD_DIM
            p0 = _mm(hp, w_in, TQ, E_COLS // 5)
            p0s = _mm(hs, w_in, bs, E_COLS // 5)
            o_nsa, o_diff = _even_attend_prompt(p0, wcat, pos2, ftab, diff_lambda[i], diff_subln[i], lam_init, t, TQ)
            nsa_new = p0s[:, E_CK * L:E_WK * L]
            win_new = p0s[:, E_WK * L:E_DQ * L]
            dk_new = p0s[:, E_DK * L:E_DV * L]
            dv_new = p0s[:, E_DV * L:E_G * L]
            heads = lambda a, n: a.reshape(bs, n, HEAD_DIM)
            o_nsa_s = _nsa_sample(
                cache_nsa_kv[i].reshape(n_pool, PAGE_SIZE * 4 * NSA_KV, HEAD_DIM), page_table,
                heads(p0s[:, :E_CK * L], NSA_HEADS), nsa_new[:, None], win_new[:, None],
                cache_nsa_win[i].reshape(bs, wb, 2 * NSA_KV * HEAD_DIM), p0s[:, E_G * L:].reshape(bs, 1, L),
                wcat, pos2, ftab, PAGES_PER_STEP).reshape(bs, nsa_w)
            o_diff_s = _diff_sample(
                cache_diff_kv[i].reshape(n_pool, PAGE_SIZE * 2 * DIFF_HEADS, HEAD_DIM), page_table,
                heads(p0s[:, E_DQ * L:E_DK * L], DIFF_HEADS), heads(dk_new, DIFF_HEADS), heads(dv_new, DIFF_HEADS),
                ftab, diff_lambda[i], diff_subln[i], lam_init, PAGES_PER_STEP).reshape(bs, DIFF_HEADS * HEAD_DIM)
            w_outs = [w_out[:nsa_w], w_out[nsa_w:]]
            hp = _mm_ln([o_nsa, o_diff], w_outs, hp, ln_g[layer, 0], ln_b[layer, 0], TM)
            hs = _mm_ln([o_nsa_s, o_diff_s], w_outs, hs, ln_g[layer, 0], ln_b[layer, 0], bs)
            p_nsa.append(p0[:, E_CK * L:E_WK * L].reshape(1, t, 4, NSA_KV, HEAD_DIM))
            p_win.append(p0[t - min(WINDOW, t):, E_WK * L:E_DQ * L].reshape(1, min(WINDOW, t), 2, NSA_KV, HEAD_DIM))
            p_diff.append(p0[:, E_DK * L:E_G * L].reshape(1, t, 2, DIFF_HEADS, HEAD_DIM))
            s_nsa.append(nsa_new.reshape(bs, 1, 4, NSA_KV, HEAD_DIM))
            s_win.append(jnp.concatenate([cache_nsa_win[i][:, 1:], win_new.reshape(bs, 1, 2, NSA_KV, HEAD_DIM)], axis=1))
            s_diff.append(jnp.concatenate([dk_new, dv_new], axis=1).reshape(bs, 1, 2, DIFF_HEADS, HEAD_DIM))
        else:
            fox_w = FOX_HEADS * HEAD_DIM
            w_in = jnp.pad(w_odd_in[i], ((0, 0), (0, O_COLS - w_odd_in.shape[2]))).astype(_MM)
            p1 = _mm(hp, w_in, TQ, O_COLS // 7)
            p1s = _mm(hs, w_in, bs, O_COLS // 7)
            logf, cum = _fox_prep(p1, b_forget[i], TM)
            logf_s, _ = _fox_prep(p1s, b_forget[i], bs)
            o = _fox_prompt(p1, cum, t, TQ)
            heads = lambda a: a.reshape(bs, FOX_HEADS, HEAD_DIM)
            o_s = _fox_sample(
                cache_fox_kv[i].reshape(n_pool, PAGE_SIZE * 2 * FOX_HEADS, HEAD_DIM),
                cache_fox_logf[i].astype(F32).transpose(0, 2, 1), page_table, heads(p1s[:, :fox_w]),
                heads(p1s[:, fox_w:2 * fox_w]), heads(p1s[:, 2 * fox_w:3 * fox_w]), logf_s[:, :FOX_HEADS, None],
                PAGES_PER_STEP // 2).reshape(bs, fox_w)
            w_out = w_odd_out[i].astype(_MM)
            hp = _mm_ln([o], [w_out], hp, ln_g[layer, 0], ln_b[layer, 0], TM)
            hs = _mm_ln([o_s], [w_out], hs, ln_g[layer, 0], ln_b[layer, 0], bs)
            p_fox.append(p1[:, fox_w:3 * fox_w].reshape(1, t, 2, FOX_HEADS, HEAD_DIM))
            p_logf.append(logf[:, :FOX_HEADS].reshape(1, t, FOX_HEADS))
            s_fox.append(p1s[:, fox_w:3 * fox_w].reshape(bs, 1, 2, FOX_HEADS, HEAD_DIM))
            s_logf.append(logf_s[:, :FOX_HEADS].reshape(bs, 1, FOX_HEADS))
        mem_kv = _mm(mem_prompt.reshape(N_MEM, d), w_mem_kv[layer].astype(_MM), N_MEM, 2 * MEM_WIDTH)
        p_mem.append(mem_kv.reshape(1, N_MEM, 2, MEM_HEADS, HEAD_DIM))
        hp, hs = _memory_layer(hp, hs, mem_kv, cache_mem_kv[layer].reshape(bs, N_MEM, 2 * MEM_WIDTH),
                               w_mem_q[layer], w_mem_o[layer], ln_g[layer, 1], ln_b[layer, 1])
        hp, hs = _moe_layer(hp, hs, w_router[layer], b_router[layer], w_exp_gate[layer], w_exp_up[layer],
                            w_exp_down[layer], w_sh_gate[layer], w_sh_up[layer], w_sh_down[layer],
                            ln_g[layer, 2], ln_b[layer, 2])
    return (hp.reshape(1, t, d), hs.reshape(bs, 1, d), jnp.stack(p_nsa), jnp.stack(p_win), jnp.stack(p_diff),
            jnp.stack(p_fox), jnp.stack(p_logf), jnp.stack(p_mem), jnp.stack(s_nsa), jnp.stack(s_win),
            jnp.stack(s_diff), jnp.stack(s_fox), jnp.stack(s_logf))
```
